```python
import math
import jax, jax.numpy as jnp
from jax import lax
import numpy as np

D_MODEL = 1024
BATCH = 4
SEQ = 4096
DEPTH = 2

A_WIDTH = D_MODEL // 2
A_GROUPS = 4
A_KERNEL = 31
B_WIDTH = D_MODEL // 2
B_GROUPS = 4
B_GROUP_DIM = B_WIDTH // B_GROUPS
B_CHUNK = 128
HEAD_DIM = 64
N_Q_HEADS = D_MODEL // HEAD_DIM
N_KV_HEADS = N_Q_HEADS // 8
Q_PER_KV = N_Q_HEADS // N_KV_HEADS
WINDOW = 128
ATT_BLOCK = 128
D_FF = 2816
FFN_KERNEL = 3
N_EVEN = (DEPTH + 1) // 2
N_ODD = DEPTH // 2
DEEPNORM_ALPHA = (2.0 * DEPTH) ** 0.25
DEEPNORM_BETA = (8.0 * DEPTH) ** -0.25
LN_EPS = 1e-5

kernel_name = "hybrid_conv_gmlp_swa_sink_deepnorm"


def layer_norm(x, g, b):
    xf = x.astype(jnp.float32)
    mu = jnp.mean(xf, axis=-1, keepdims=True)
    var = jnp.mean(jnp.square(xf - mu), axis=-1, keepdims=True)
    y = (xf - mu) * lax.rsqrt(var + LN_EPS)
    return (y * g.astype(jnp.float32) + b.astype(jnp.float32)).astype(x.dtype)


def causal_dwconv(x, w, b):
    k = w.shape[0]
    c = x.shape[-1]
    y = lax.conv_general_dilated(
        x, w[:, None, :].astype(x.dtype), window_strides=(1,), padding=[(k - 1, 0)],
        dimension_numbers=("NWC", "WIO", "NWC"), feature_group_count=c)
    return y + b.astype(x.dtype)


def conv_gmlp_mixer(x, w_in, a_conv_w, a_conv_b, a_norm_g, a_norm_b,
                    b_norm_g, b_norm_b, b_spatial_w, b_spatial_b, w_out):
    bsz, seq, _ = x.shape
    h = x @ w_in
    a_val, a_gate, b_u, b_v = jnp.split(h, 4, axis=-1)
    a = a_val * jax.nn.sigmoid(a_gate)
    a = causal_dwconv(a, a_conv_w, a_conv_b)
    a = jax.nn.silu(layer_norm(a, a_norm_g, a_norm_b))
    b_u = jax.nn.gelu(b_u)
    b_v = layer_norm(jax.nn.gelu(b_v), b_norm_g, b_norm_b)
    n_chunks = seq // B_CHUNK
    vb = b_v.reshape(bsz, n_chunks, B_CHUNK, B_GROUPS, B_GROUP_DIM)
    causal = jnp.tril(jnp.ones((B_CHUNK, B_CHUNK), dtype=bool))
    ws = jnp.where(causal[None], b_spatial_w, jnp.zeros((), b_spatial_w.dtype))
    mixed = jnp.einsum("gij,bcjgd->bcigd", ws, vb) + b_spatial_b.T[None, None, :, :, None]
    b = b_u * mixed.reshape(bsz, seq, B_WIDTH)
    return jnp.concatenate([a, b], axis=-1) @ w_out


def swa_sink_attention(x, w_qkv, b_qkv, sinks, w_o):
    bsz, seq, _ = x.shape
    nb = seq // ATT_BLOCK
    qkv = x @ w_qkv + b_qkv
    q, k, v = jnp.split(qkv, [N_Q_HEADS * HEAD_DIM, (N_Q_HEADS + N_KV_HEADS) * HEAD_DIM], axis=-1)
    q = q.reshape(bsz, nb, ATT_BLOCK, N_KV_HEADS, Q_PER_KV, HEAD_DIM)
    k = k.reshape(bsz, nb, ATT_BLOCK, N_KV_HEADS, HEAD_DIM)
    v = v.reshape(bsz, nb, ATT_BLOCK, N_KV_HEADS, HEAD_DIM)

    def with_prev(t):
        prev = jnp.concatenate([jnp.zeros_like(t[:, :1]), t[:, :-1]], axis=1)
        return jnp.concatenate([prev, t], axis=2)

    kk, vv = with_prev(k), with_prev(v)
    scale = 1.0 / math.sqrt(HEAD_DIM)
    scores = jnp.einsum("bnqkgd,bnskd->bnkgqs", q, kk).astype(jnp.float32) * scale
    qi = jnp.arange(ATT_BLOCK)[:, None]
    sj = jnp.arange(2 * ATT_BLOCK)[None, :]
    diff = qi + ATT_BLOCK - sj
    in_window = (diff >= 0) & (diff < WINDOW)
    blk = jnp.arange(nb)[:, None, None]
    valid = in_window[None] & ((blk > 0) | (sj[None] >= ATT_BLOCK))
    scores = jnp.where(valid[None, :, None, None], scores, -jnp.inf)
    sink = jnp.broadcast_to(sinks.astype(jnp.float32).reshape(1, 1, N_KV_HEADS, Q_PER_KV, 1, 1),
                            scores.shape[:-1] + (1,))
    probs = jax.nn.softmax(jnp.concatenate([scores, sink], axis=-1), axis=-1)[..., :-1]
    out = jnp.einsum("bnkgqs,bnskd->bnqkgd", probs.astype(vv.dtype), vv)
    return out.reshape(bsz, seq, N_Q_HEADS * HEAD_DIM) @ w_o


def conv_ffn(x, w_up, conv_w, conv_b, w_down):
    h = causal_dwconv(x @ w_up, conv_w, conv_b)
    gate, val = jnp.split(h, 2, axis=-1)
    return (jax.nn.gelu(gate) * val) @ w_down


def setup_inputs(seed: int = 0) -> dict:
    key = jax.random.key(seed)
    ks = jax.random.split(key, 24)
    f32 = jnp.float32
    nrm = lambda k, shape, s: jax.random.normal(k, shape, f32) * s
    d_in_ab = 2 * A_WIDTH + 2 * B_WIDTH
    d_qkv = (N_Q_HEADS + 2 * N_KV_HEADS) * HEAD_DIM
    return {
        "x": nrm(ks[0], (BATCH, SEQ, D_MODEL), 1.0),
        "ab_w_in": nrm(ks[1], (N_EVEN, D_MODEL, d_in_ab), D_MODEL ** -0.5),
        "a_conv_w": nrm(ks[2], (N_EVEN, A_KERNEL, A_WIDTH), A_KERNEL ** -0.5),
        "a_conv_b": nrm(ks[3], (N_EVEN, A_WIDTH), 0.02),
        "a_norm_g": 1.0 + nrm(ks[4], (N_EVEN, A_WIDTH), 0.02),
        "a_norm_b": nrm(ks[5], (N_EVEN, A_WIDTH), 0.02),
        "b_norm_g": 1.0 + nrm(ks[6], (N_EVEN, B_WIDTH), 0.02),
        "b_norm_b": nrm(ks[7], (N_EVEN, B_WIDTH), 0.02),
        "b_spatial_w": nrm(ks[8], (N_EVEN, B_GROUPS, B_CHUNK, B_CHUNK), B_CHUNK ** -0.5),
        "b_spatial_b": 1.0 + nrm(ks[9], (N_EVEN, B_GROUPS, B_CHUNK), 0.02),
        "ab_w_out": nrm(ks[10], (N_EVEN, A_WIDTH + B_WIDTH, D_MODEL), (A_WIDTH + B_WIDTH) ** -0.5 * DEEPNORM_BETA),
        "c_w_qkv": nrm(ks[11], (N_ODD, D_MODEL, d_qkv), D_MODEL ** -0.5),
        "c_b_qkv": nrm(ks[12], (N_ODD, d_qkv), 0.02),
        "c_sinks": nrm(ks[13], (N_ODD, N_Q_HEADS), 0.5),
        "c_w_o": nrm(ks[14], (N_ODD, N_Q_HEADS * HEAD_DIM, D_MODEL), (N_Q_HEADS * HEAD_DIM) ** -0.5 * DEEPNORM_BETA),
        "ffn_w_up": nrm(ks[15], (DEPTH, D_MODEL, 2 * D_FF), D_MODEL ** -0.5),
        "ffn_conv_w": nrm(ks[16], (DEPTH, FFN_KERNEL, 2 * D_FF), FFN_KERNEL ** -0.5),
        "ffn_conv_b": nrm(ks[17], (DEPTH, 2 * D_FF), 0.02),
        "ffn_w_down": nrm(ks[18], (DEPTH, D_FF, D_MODEL), D_FF ** -0.5 * DEEPNORM_BETA),
        "ln_g": 1.0 + nrm(ks[19], (DEPTH, 2, D_MODEL), 0.02),
        "ln_b": nrm(ks[20], (DEPTH, 2, D_MODEL), 0.02),
    }


def reference(x, ab_w_in, a_conv_w, a_conv_b, a_norm_g, a_norm_b, b_norm_g, b_norm_b,
              b_spatial_w, b_spatial_b, ab_w_out, c_w_qkv, c_b_qkv, c_sinks, c_w_o,
              ffn_w_up, ffn_conv_w, ffn_conv_b, ffn_w_down, ln_g, ln_b):
    alpha = jnp.asarray(DEEPNORM_ALPHA, dtype=x.dtype)
    for i in range(DEPTH):
        j = i // 2
        if i % 2 == 0:
            mix = conv_gmlp_mixer(x, ab_w_in[j], a_conv_w[j], a_conv_b[j], a_norm_g[j], a_norm_b[j],
                                  b_norm_g[j], b_norm_b[j], b_spatial_w[j], b_spatial_b[j], ab_w_out[j])
        else:
            mix = swa_sink_attention(x, c_w_qkv[j], c_b_qkv[j], c_sinks[j], c_w_o[j])
        x = layer_norm(alpha * x + mix, ln_g[i, 0], ln_b[i, 0])
        ffn = conv_ffn(x, ffn_w_up[i], ffn_conv_w[i], ffn_conv_b[i], ffn_w_down[i])
        x = layer_norm(alpha * x + ffn, ln_g[i, 1], ln_b[i, 1])
    return x
```

```python
import functools
import math

import jax
import jax.numpy as jnp
import numpy as np
from jax import lax
from jax.experimental import pallas as pl
from jax.experimental.pallas import tpu as pltpu

F32 = jnp.float32
BF16 = jnp.bfloat16

LN_EPS = 1e-5
SUBLANES = 8
LANES = 128
GMLP_CHUNK = 128
GMLP_GROUPS = 4
ATT_BLOCK = 128
HEAD_DIM = 64
Q_PER_KV = 8
N_KV_HEADS = 2
A_TAIL = 32
VMEM_LIMIT_BYTES = 56 * 1024 * 1024

ROW_TILE = 512
FFN_COL_TILE = 256


def _layer_norm(z, g, b):
    mu = jnp.mean(z, axis=-1, keepdims=True)
    d = z - mu
    var = jnp.mean(d * d, axis=-1, keepdims=True)
    return d * lax.rsqrt(var + LN_EPS) * g + b


def _dot(a, b):
    return jnp.dot(a, b, preferred_element_type=F32)


def _shift_rows(h, prev_tail, k):
    rolled = pltpu.roll(h, k, axis=0)
    tail = pltpu.roll(prev_tail, k, axis=0)
    row = lax.broadcasted_iota(jnp.int32, tail.shape, 0)
    head = jnp.where(row < k, tail, rolled[:SUBLANES])
    return jnp.concatenate([head, rolled[SUBLANES:]], axis=0)


def _const_spec(shape):
    nd = len(shape)
    return pl.BlockSpec(shape, lambda i: (0,) * nd, pipeline_mode=pl.Buffered(1))


def _ffn_kernel(x_ref, wup_ref, cw_ref, cb_ref, wdn_ref, g_ref, b_ref, o_ref,
                act_ref, carry_ref, *, alpha, tiles_per_batch, d_ff, col_tile):
    tm = x_ref.shape[0]

    @pl.when(pl.program_id(0) % tiles_per_batch == 0)
    def _():
        carry_ref[...] = jnp.zeros_like(carry_ref)

    x = x_ref[...]
    xb = x.astype(BF16)

    def conv_chunk(col):
        h = _dot(xb, wup_ref[:, col:col + col_tile])
        prev = carry_ref[:, col:col + col_tile]
        carry_ref[:, col:col + col_tile] = h[tm - SUBLANES:]
        w = cw_ref[:, col:col + col_tile]
        y = w[2:3] * h + cb_ref[:, col:col + col_tile]
        y = y + w[1:2] * _shift_rows(h, prev, 1)
        y = y + w[0:1] * _shift_rows(h, prev, 2)
        return y

    for c in range(d_ff // col_tile):
        gate = conv_chunk(c * col_tile)
        val = conv_chunk(d_ff + c * col_tile)
        act = jax.nn.gelu(gate, approximate=True) * val
        act_ref[:, c * col_tile:(c + 1) * col_tile] = act.astype(BF16)

    ffn = _dot(act_ref[...], wdn_ref[...])
    o_ref[...] = _layer_norm(alpha * x + ffn, g_ref[...], b_ref[...])


def _ffn_call(x2d, w_up, conv_w, conv_b, w_down, ln_g, ln_b, *, alpha, seq):
    n, d = x2d.shape
    d_ff = w_down.shape[0]
    tm = ROW_TILE
    kern = functools.partial(_ffn_kernel, alpha=alpha, tiles_per_batch=seq // tm,
                             d_ff=d_ff, col_tile=FFN_COL_TILE)
    return pl.pallas_call(
        kern,
        grid=(n // tm,),
        in_specs=[
            pl.BlockSpec((tm, d), lambda i: (i, 0)),
            _const_spec((d, 2 * d_ff)),
            _const_spec((3, 2 * d_ff)),
            _const_spec((1, 2 * d_ff)),
            _const_spec((d_ff, d)),
            _const_spec((1, d)),
            _const_spec((1, d)),
        ],
        out_specs=pl.BlockSpec((tm, d), lambda i: (i, 0)),
        out_shape=jax.ShapeDtypeStruct((n, d), F32),
        scratch_shapes=[
            pltpu.VMEM((tm, d_ff), BF16),
            pltpu.VMEM((SUBLANES, 2 * d_ff), F32),
        ],
        compiler_params=pltpu.CompilerParams(
            dimension_semantics=("arbitrary",), vmem_limit_bytes=VMEM_LIMIT_BYTES),
        name="conv_ffn",
    )(x2d, w_up.astype(BF16), conv_w, conv_b.reshape(1, -1), w_down.astype(BF16),
      ln_g.reshape(1, -1), ln_b.reshape(1, -1))


def _mixer_kernel(x_ref, win_ref, acw_ref, acb_ref, ang_ref, anb_ref, bng_ref, bnb_ref,
                  ws_ref, sb_ref, wout_ref, g_ref, b_ref, o_ref, carry_ref,
                  *, alpha, tiles_per_batch, width):
    tm = x_ref.shape[0]

    @pl.when(pl.program_id(0) % tiles_per_batch == 0)
    def _():
        carry_ref[...] = jnp.zeros_like(carry_ref)

    x = x_ref[...]
    xb = x.astype(BF16)
    a_val = _dot(xb, win_ref[:, 0 * width:1 * width])
    a_gate = _dot(xb, win_ref[:, 1 * width:2 * width])
    b_u = _dot(xb, win_ref[:, 2 * width:3 * width])
    b_v = _dot(xb, win_ref[:, 3 * width:4 * width])

    a = a_val * jax.nn.sigmoid(a_gate)
    ext = jnp.concatenate([carry_ref[...], a], axis=0)
    carry_ref[...] = a[tm - A_TAIL:]
    n_taps = acw_ref.shape[0]
    acw = acw_ref[...]
    conv = jnp.broadcast_to(acb_ref[...], (tm, width))
    for r in range(SUBLANES):
        ext_r = ext if r == 0 else pltpu.roll(ext, r, axis=0)
        for q in range(A_TAIL // SUBLANES):
            s = SUBLANES * q + r
            if s >= n_taps:
                continue
            j = n_taps - 1 - s
            start = A_TAIL - SUBLANES * q
            conv = conv + acw[j:j + 1] * ext_r[start:start + tm]
    a_out = _layer_norm(conv, ang_ref[...], anb_ref[...])
    a_out = a_out * jax.nn.sigmoid(a_out)

    u = jax.nn.gelu(b_u, approximate=True)
    v = _layer_norm(jax.nn.gelu(b_v, approximate=True), bng_ref[...], bnb_ref[...]).astype(BF16)
    ri = lax.broadcasted_iota(jnp.int32, (GMLP_CHUNK, GMLP_CHUNK), 0)
    ci = lax.broadcasted_iota(jnp.int32, (GMLP_CHUNK, GMLP_CHUNK), 1)
    gd = width // GMLP_GROUPS
    ws = [jnp.where(ri >= ci, ws_ref[g], 0.0).astype(BF16) for g in range(GMLP_GROUPS)]
    sbias = sb_ref[...]
    chunks = []
    for c in range(tm // GMLP_CHUNK):
        rows = slice(c * GMLP_CHUNK, (c + 1) * GMLP_CHUNK)
        cols = [_dot(ws[g], v[rows, g * gd:(g + 1) * gd]) for g in range(GMLP_GROUPS)]
        chunks.append(jnp.concatenate(cols, axis=1) + sbias)
    b_out = u * jnp.concatenate(chunks, axis=0)

    mix = _dot(jnp.concatenate([a_out.astype(BF16), b_out.astype(BF16)], axis=1), wout_ref[...])
    o_ref[...] = _layer_norm(alpha * x + mix, g_ref[...], b_ref[...])


def _mixer_call(x2d, w_in, a_conv_w, a_conv_b, a_norm_g, a_norm_b, b_norm_g, b_norm_b,
                b_spatial_w, b_spatial_b, w_out, ln_g, ln_b, *, alpha, seq):
    n, d = x2d.shape
    width = a_conv_w.shape[1]
    gd = width // GMLP_GROUPS
    tm = ROW_TILE
    sbias = jnp.repeat(b_spatial_b.T, gd, axis=1)
    kern = functools.partial(_mixer_kernel, alpha=alpha, tiles_per_batch=seq // tm, width=width)
    row = lambda a: a.reshape(1, -1)
    return pl.pallas_call(
        kern,
        grid=(n // tm,),
        in_specs=[
            pl.BlockSpec((tm, d), lambda i: (i, 0)),
            _const_spec(w_in.shape),
            _const_spec(a_conv_w.shape),
            _const_spec((1, width)),
            _const_spec((1, width)),
            _const_spec((1, width)),
            _const_spec((1, width)),
            _const_spec((1, width)),
            _const_spec(b_spatial_w.shape),
            _const_spec(sbias.shape),
            _const_spec(w_out.shape),
            _const_spec((1, d)),
            _const_spec((1, d)),
        ],
        out_specs=pl.BlockSpec((tm, d), lambda i: (i, 0)),
        out_shape=jax.ShapeDtypeStruct((n, d), F32),
        scratch_shapes=[pltpu.VMEM((A_TAIL, width), F32)],
        compiler_params=pltpu.CompilerParams(
            dimension_semantics=("arbitrary",), vmem_limit_bytes=VMEM_LIMIT_BYTES),
        name="conv_gmlp_mixer",
    )(x2d, w_in.astype(BF16), a_conv_w, row(a_conv_b), row(a_norm_g), row(a_norm_b),
      row(b_norm_g), row(b_norm_b), b_spatial_w, sbias, w_out.astype(BF16), row(ln_g), row(ln_b))


def _attn_kernel(sinks_ref, x_ref, wqkv_ref, bqkv_ref, wo_ref, g_ref, b_ref, o_ref,
                 attn_ref, kv_carry_ref, *, alpha, tiles_per_batch):
    tm = x_ref.shape[0]
    d_q = N_KV_HEADS * Q_PER_KV * HEAD_DIM
    d_kv = N_KV_HEADS * HEAD_DIM
    pair_w = 2 * HEAD_DIM
    pairs = Q_PER_KV // 2
    first = pl.program_id(0) % tiles_per_batch == 0

    @pl.when(first)
    def _():
        kv_carry_ref[...] = jnp.zeros_like(kv_carry_ref)

    x = x_ref[...]
    qkv = _dot(x.astype(BF16), wqkv_ref[...]) + bqkv_ref[...]
    q = (qkv[:, :d_q] * (1.0 / math.sqrt(HEAD_DIM))).astype(BF16)
    kv = qkv[:, d_q:]
    kv_all = jnp.concatenate([kv_carry_ref[...], kv], axis=0)
    kv_carry_ref[...] = kv[tm - ATT_BLOCK:]
    k_all = kv_all[:, :d_kv]
    v_all = kv_all[:, d_kv:]
    k_sw = pltpu.roll(k_all, HEAD_DIM, axis=1)
    v_sw = pltpu.roll(v_all, HEAD_DIM, axis=1)
    low = lax.broadcasted_iota(jnp.int32, k_all.shape, 1) < HEAD_DIM

    qi = lax.broadcasted_iota(jnp.int32, (ATT_BLOCK, 2 * ATT_BLOCK), 0)
    sj = lax.broadcasted_iota(jnp.int32, (ATT_BLOCK, 2 * ATT_BLOCK), 1)
    diff = qi + ATT_BLOCK - sj
    in_window = (diff >= 0) & (diff < ATT_BLOCK)
    first_valid = in_window & (sj >= jnp.where(first, ATT_BLOCK, 0))
    nt = (((1,), (1,)), ((), ()))

    for kvh in range(N_KV_HEADS):
        k_src, k_alt = (k_all, k_sw) if kvh == 0 else (k_sw, k_all)
        v_src, v_alt = (v_all, v_sw) if kvh == 0 else (v_sw, v_all)
        k_even = jnp.where(low, k_src, 0.0).astype(BF16)
        k_odd = jnp.where(low, 0.0, k_alt).astype(BF16)
        v_even = jnp.where(low, v_src, 0.0).astype(BF16)
        v_odd = jnp.where(low, 0.0, v_alt).astype(BF16)
        for n in range(tm // ATT_BLOCK):
            kr = slice(n * ATT_BLOCK, (n + 2) * ATT_BLOCK)
            qr = slice(n * ATT_BLOCK, (n + 1) * ATT_BLOCK)
            keys = jnp.concatenate([k_even[kr], k_odd[kr]], axis=0)
            vals = jnp.concatenate([v_even[kr], v_odd[kr]], axis=0)
            col0 = kvh * Q_PER_KV * HEAD_DIM
            qs = jnp.concatenate(
                [q[qr, col0 + p * pair_w:col0 + (p + 1) * pair_w] for p in range(pairs)], axis=0)
            s = lax.dot_general(qs, keys, nt, preferred_element_type=F32)
            valid = first_valid if n == 0 else in_window
            prob_rows = []
            for p in range(pairs):
                halves = []
                for e in range(2):
                    blk = s[p * ATT_BLOCK:(p + 1) * ATT_BLOCK,
                            e * 2 * ATT_BLOCK:(e + 1) * 2 * ATT_BLOCK]
                    blk = jnp.where(valid, blk, -jnp.inf)
                    sink = sinks_ref[kvh * Q_PER_KV + 2 * p + e]
                    m = jnp.maximum(jnp.max(blk, axis=1, keepdims=True), sink)
                    pe = jnp.exp(blk - m)
                    den = jnp.sum(pe, axis=1, keepdims=True) + jnp.exp(sink - m)
                    halves.append((pe / den).astype(BF16))
                prob_rows.append(jnp.concatenate(halves, axis=1))
            probs = jnp.concatenate(prob_rows, axis=0)
            o = _dot(probs, vals)
            for p in range(pairs):
                attn_ref[qr, col0 + p * pair_w:col0 + (p + 1) * pair_w] = (
                    o[p * ATT_BLOCK:(p + 1) * ATT_BLOCK].astype(BF16))

    mix = _dot(attn_ref[...], wo_ref[...])
    o_ref[...] = _layer_norm(alpha * x + mix, g_ref[...], b_ref[...])


def _attn_call(x2d, w_qkv, b_qkv, sinks, w_o, ln_g, ln_b, *, alpha, seq):
    n, d = x2d.shape
    d_q = w_o.shape[0]
    tm = ROW_TILE
    kern = functools.partial(_attn_kernel, alpha=alpha, tiles_per_batch=seq // tm)
    return pl.pallas_call(
        kern,
        grid=(n // tm,),
        in_specs=[
            pl.BlockSpec(memory_space=pltpu.SMEM),
            pl.BlockSpec((tm, d), lambda i: (i, 0)),
            _const_spec(w_qkv.shape),
            _const_spec((1, w_qkv.shape[1])),
            _const_spec(w_o.shape),
            _const_spec((1, d)),
            _const_spec((1, d)),
        ],
        out_specs=pl.BlockSpec((tm, d), lambda i: (i, 0)),
        out_shape=jax.ShapeDtypeStruct((n, d), F32),
        scratch_shapes=[
            pltpu.VMEM((tm, d_q), BF16),
            pltpu.VMEM((ATT_BLOCK, 2 * N_KV_HEADS * HEAD_DIM), F32),
        ],
        compiler_params=pltpu.CompilerParams(
            dimension_semantics=("arbitrary",), vmem_limit_bytes=VMEM_LIMIT_BYTES),
        name="swa_sink_attention",
    )(sinks, x2d, w_qkv.astype(BF16), b_qkv.reshape(1, -1), w_o.astype(BF16),
      ln_g.reshape(1, -1), ln_b.reshape(1, -1))


def kernel(x, ab_w_in, a_conv_w, a_conv_b, a_norm_g, a_norm_b, b_norm_g, b_norm_b, b_spatial_w, b_spatial_b, ab_w_out, c_w_qkv, c_b_qkv, c_sinks, c_w_o, ffn_w_up, ffn_conv_w, ffn_conv_b, ffn_w_down, ln_g, ln_b):
    bsz, seq, d = x.shape
    depth = ffn_w_up.shape[0]
    alpha = float(np.float32((2.0 * depth) ** 0.25))
    h = x.reshape(bsz * seq, d)
    for i in range(depth):
        j = i // 2
        if i % 2 == 0:
            h = _mixer_call(h, ab_w_in[j], a_conv_w[j], a_conv_b[j], a_norm_g[j], a_norm_b[j],
                            b_norm_g[j], b_norm_b[j], b_spatial_w[j], b_spatial_b[j], ab_w_out[j],
                            ln_g[i, 0], ln_b[i, 0], alpha=alpha, seq=seq)
        else:
            h = _attn_call(h, c_w_qkv[j], c_b_qkv[j], c_sinks[j], c_w_o[j],
                           ln_g[i, 0], ln_b[i, 0], alpha=alpha, seq=seq)
        h = _ffn_call(h, ffn_w_up[i], ffn_conv_w[i], ffn_conv_b[i], ffn_w_down[i],
                      ln_g[i, 1], ln_b[i, 1], alpha=alpha, seq=seq)
    return h.reshape(bsz, seq, d)
```

```python
import functools
import math

import jax
import jax.numpy as jnp
import numpy as np
from jax import lax
from jax.experimental import pallas as pl
from jax.experimental.pallas import tpu as pltpu

F32 = jnp.float32
BF16 = jnp.bfloat16

LN_EPS = 1e-5
SUBLANES = 8
LANES = 128
GROUP = 128
VROWS = GROUP // SUBLANES
GMLP_GROUPS = 4
HEAD_DIM = 64
Q_PER_KV = 8
N_KV_HEADS = 2
VMEM_LIMIT_BYTES = 56 * 1024 * 1024

ROW_TILE = 512
FFN_COL_TILE = 256

GELU_C0 = math.sqrt(2.0 / math.pi)
GELU_C1 = 0.044715 * GELU_C0


def _layer_norm(z, g, b):
    mu = jnp.mean(z, axis=-1, keepdims=True)
    d = z - mu
    var = jnp.mean(d * d, axis=-1, keepdims=True)
    return d * lax.rsqrt(var + LN_EPS) * g + b


def _dot(a, b):
    return jnp.dot(a, b, preferred_element_type=F32)


def _time_of_row(rho):
    return VROWS * (rho & (SUBLANES - 1)) + (rho >> 3)


def _perm_index():
    rho = np.arange(GROUP)
    return VROWS * (rho % SUBLANES) + rho // SUBLANES


def _sublane_shift(cur, prev):
    row = lax.broadcasted_iota(jnp.int32, cur.shape, 0)
    return jnp.where(row == 0, pltpu.roll(prev, 1, axis=0), pltpu.roll(cur, 1, axis=0))


def _const_spec(shape):
    nd = len(shape)
    return pl.BlockSpec(shape, lambda i: (0,) * nd, pipeline_mode=pl.Buffered(1))


def _ffn_kernel(x_ref, wup_ref, cw_ref, cb_ref, wdn_ref, g_ref, b_ref, o_ref,
                act_ref, carry_ref, *rest, alpha, tiles_per_batch, d_ff, col_tile, natural_out):
    tm = x_ref.shape[0]
    n_groups = tm // GROUP

    @pl.when(pl.program_id(0) % tiles_per_batch == 0)
    def _():
        carry_ref[...] = jnp.zeros_like(carry_ref)

    x = x_ref[...]
    xb = x.astype(BF16)

    def conv_chunk(col, scale):
        h = _dot(xb, wup_ref[:, col:col + col_tile])
        w = cw_ref[:, col:col + col_tile]
        bias = cb_ref[:, col:col + col_tile]
        if scale != 1.0:
            w, bias = w * scale, bias * scale
        prev = [carry_ref[0:SUBLANES, col:col + col_tile],
                carry_ref[SUBLANES:2 * SUBLANES, col:col + col_tile]]
        h1, h2 = [], []
        for g in range(n_groups):
            hg = h[g * GROUP:(g + 1) * GROUP]
            cur = [hg[GROUP - 2 * SUBLANES:GROUP - SUBLANES], hg[GROUP - SUBLANES:]]
            wrapped = [_sublane_shift(cur[i], prev[i]) for i in range(2)]
            h1 += [wrapped[1], hg[:GROUP - SUBLANES]]
            h2 += [wrapped[0], wrapped[1], hg[:GROUP - 2 * SUBLANES]]
            prev = cur
        carry_ref[0:SUBLANES, col:col + col_tile] = prev[0]
        carry_ref[SUBLANES:2 * SUBLANES, col:col + col_tile] = prev[1]
        y = w[2:3] * h + bias
        y = y + w[1:2] * jnp.concatenate(h1, axis=0)
        y = y + w[0:1] * jnp.concatenate(h2, axis=0)
        return y

    for c in range(d_ff // col_tile):
        gate = conv_chunk(c * col_tile, 1.0)
        half_val = conv_chunk(d_ff + c * col_tile, 0.5)
        inner = gate * (GELU_C0 + GELU_C1 * (gate * gate))
        act = gate * (1.0 + jnp.tanh(inner)) * half_val
        act_ref[:, c * col_tile:(c + 1) * col_tile] = act.astype(BF16)

    ffn = _dot(act_ref[...], wdn_ref[...])
    y = _layer_norm(alpha * x + ffn, g_ref[...], b_ref[...])
    if not natural_out:
        o_ref[...] = y
    else:
        (stage_ref,) = rest
        d = y.shape[1]
        for c in range(d // LANES):
            for g in range(n_groups):
                for j in range(VROWS):
                    r0 = g * GROUP + j * SUBLANES
                    stage_ref[c, pl.ds(g * GROUP + j, SUBLANES, stride=VROWS), :] = (
                        y[r0:r0 + SUBLANES, c * LANES:(c + 1) * LANES])
        for c in range(d // LANES):
            o_ref[:, c * LANES:(c + 1) * LANES] = stage_ref[c]


def _ffn_call(x2d, w_up, conv_w, conv_b, w_down, ln_g, ln_b, *, alpha, seq, natural_out):
    n, d = x2d.shape
    d_ff = w_down.shape[0]
    tm = ROW_TILE
    kern = functools.partial(_ffn_kernel, alpha=alpha, tiles_per_batch=seq // tm,
                             d_ff=d_ff, col_tile=FFN_COL_TILE, natural_out=natural_out)
    scratch = [pltpu.VMEM((tm, d_ff), BF16), pltpu.VMEM((2 * SUBLANES, 2 * d_ff), F32)]
    if natural_out:
        scratch.append(pltpu.VMEM((d // LANES, tm, LANES), F32))
    return pl.pallas_call(
        kern,
        grid=(n // tm,),
        in_specs=[
            pl.BlockSpec((tm, d), lambda i: (i, 0)),
            _const_spec((d, 2 * d_ff)),
            _const_spec((3, 2 * d_ff)),
            _const_spec((1, 2 * d_ff)),
            _const_spec((d_ff, d)),
            _const_spec((1, d)),
            _const_spec((1, d)),
        ],
        out_specs=pl.BlockSpec((tm, d), lambda i: (i, 0)),
        out_shape=jax.ShapeDtypeStruct((n, d), F32),
        scratch_shapes=scratch,
        compiler_params=pltpu.CompilerParams(
            dimension_semantics=("arbitrary",), vmem_limit_bytes=VMEM_LIMIT_BYTES),
        name="conv_ffn_out" if natural_out else "conv_ffn",
    )(x2d, w_up.astype(BF16), conv_w, conv_b.reshape(1, -1), w_down.astype(BF16),
      ln_g.reshape(1, -1), ln_b.reshape(1, -1))


def _mixer_kernel(*refs, alpha, tiles_per_batch, width, n_col_blocks):
    x_refs = refs[:n_col_blocks]
    (win_ref, acw_ref, acb_ref, ang_ref, anb_ref, bng_ref, bnb_ref,
     ws_ref, sb_ref, wout_ref, g_ref, b_ref, o_ref, carry_ref) = refs[n_col_blocks:]
    tm = o_ref.shape[0]
    n_groups = tm // GROUP

    @pl.when(pl.program_id(0) % tiles_per_batch == 0)
    def _():
        carry_ref[...] = jnp.zeros_like(carry_ref)

    x = jnp.concatenate(
        [jnp.concatenate(
            [x_refs[c][pl.ds(g * GROUP + j, SUBLANES, stride=VROWS), :]
             for g in range(n_groups) for j in range(VROWS)], axis=0)
         for c in range(n_col_blocks)], axis=1)
    xb = x.astype(BF16)
    a_val = _dot(xb, win_ref[:, 0 * width:1 * width])
    a_gate = _dot(xb, win_ref[:, 1 * width:2 * width])
    b_u = _dot(xb, win_ref[:, 2 * width:3 * width])
    b_v = _dot(xb, win_ref[:, 3 * width:4 * width])

    a = a_val * jax.nn.sigmoid(a_gate)
    ext = jnp.concatenate([carry_ref[...], a], axis=0)
    carry_ref[...] = a[tm - GROUP:]
    n_taps = acw_ref.shape[0]
    max_sub = (n_taps - 1) // VROWS + 1

    def sublane_shifted(z):
        out = []
        for g in range(n_groups + 1):
            zg = z[g * GROUP:(g + 1) * GROUP]
            zp = z[(g - 1) * GROUP:g * GROUP] if g > 0 else zg
            out += [_sublane_shift(zg[j * SUBLANES:(j + 1) * SUBLANES],
                                   zp[j * SUBLANES:(j + 1) * SUBLANES]) for j in range(VROWS)]
        return jnp.concatenate(out, axis=0)

    shifted = [ext]
    for _ in range(max_sub):
        shifted.append(sublane_shifted(shifted[-1]))
    acw = acw_ref[...]
    conv = jnp.broadcast_to(acb_ref[...], (tm, width))
    for k in range(n_taps):
        q, r = divmod(k, VROWS)
        lo = shifted[q][GROUP:]
        if r == 0:
            term = lo
        else:
            hi = shifted[q + 1][GROUP:]
            cut = (VROWS - r) * SUBLANES
            parts = []
            for g in range(n_groups):
                parts += [hi[g * GROUP + cut:(g + 1) * GROUP], lo[g * GROUP:g * GROUP + cut]]
            term = jnp.concatenate(parts, axis=0)
        j = n_taps - 1 - k
        conv = conv + acw[j:j + 1] * term
    a_out = _layer_norm(conv, ang_ref[...], anb_ref[...])
    a_out = a_out * jax.nn.sigmoid(a_out)

    u = jax.nn.gelu(b_u, approximate=True)
    v = _layer_norm(jax.nn.gelu(b_v, approximate=True), bng_ref[...], bnb_ref[...]).astype(BF16)
    ti = _time_of_row(lax.broadcasted_iota(jnp.int32, (GROUP, GROUP), 0))
    tj = _time_of_row(lax.broadcasted_iota(jnp.int32, (GROUP, GROUP), 1))
    gd = width // GMLP_GROUPS
    ws = [jnp.where(ti >= tj, ws_ref[g], 0.0).astype(BF16) for g in range(GMLP_GROUPS)]
    sbias = sb_ref[...]
    chunks = []
    for c in range(n_groups):
        rows = slice(c * GROUP, (c + 1) * GROUP)
        cols = [_dot(ws[g], v[rows, g * gd:(g + 1) * gd]) for g in range(GMLP_GROUPS)]
        chunks.append(jnp.concatenate(cols, axis=1) + sbias)
    b_out = u * jnp.concatenate(chunks, axis=0)

    mix = _dot(jnp.concatenate([a_out.astype(BF16), b_out.astype(BF16)], axis=1), wout_ref[...])
    o_ref[...] = _layer_norm(alpha * x + mix, g_ref[...], b_ref[...])


def _mixer_call(x2d, w_in, a_conv_w, a_conv_b, a_norm_g, a_norm_b, b_norm_g, b_norm_b,
                b_spatial_w, b_spatial_b, w_out, ln_g, ln_b, *, alpha, seq):
    n, d = x2d.shape
    width = a_conv_w.shape[1]
    gd = width // GMLP_GROUPS
    tm = ROW_TILE
    n_col_blocks = d // LANES
    perm = _perm_index()
    ws_perm = b_spatial_w[:, perm][:, :, perm]
    sbias = jnp.repeat(b_spatial_b.T[perm], gd, axis=1)
    kern = functools.partial(_mixer_kernel, alpha=alpha, tiles_per_batch=seq // tm, width=width,
                             n_col_blocks=n_col_blocks)
    row = lambda a: a.reshape(1, -1)
    x_specs = [pl.BlockSpec((tm, LANES), functools.partial(lambda c, i: (i, c), c))
               for c in range(n_col_blocks)]
    return pl.pallas_call(
        kern,
        grid=(n // tm,),
        in_specs=x_specs + [
            _const_spec(w_in.shape),
            _const_spec(a_conv_w.shape),
            _const_spec((1, width)),
            _const_spec((1, width)),
            _const_spec((1, width)),
            _const_spec((1, width)),
            _const_spec((1, width)),
            _const_spec(b_spatial_w.shape),
            _const_spec(sbias.shape),
            _const_spec(w_out.shape),
            _const_spec((1, d)),
            _const_spec((1, d)),
        ],
        out_specs=pl.BlockSpec((tm, d), lambda i: (i, 0)),
        out_shape=jax.ShapeDtypeStruct((n, d), F32),
        scratch_shapes=[pltpu.VMEM((GROUP, width), F32)],
        compiler_params=pltpu.CompilerParams(
            dimension_semantics=("arbitrary",), vmem_limit_bytes=VMEM_LIMIT_BYTES),
        name="conv_gmlp_mixer",
    )(*([x2d] * n_col_blocks), w_in.astype(BF16), a_conv_w, row(a_conv_b), row(a_norm_g),
      row(a_norm_b), row(b_norm_g), row(b_norm_b), ws_perm, sbias, w_out.astype(BF16),
      row(ln_g), row(ln_b))


def _attn_kernel(sinks_ref, x_ref, wqkv_ref, bqkv_ref, wo_ref, g_ref, b_ref, o_ref,
                 attn_ref, kv_carry_ref, *, alpha, tiles_per_batch):
    tm = x_ref.shape[0]
    d_q = N_KV_HEADS * Q_PER_KV * HEAD_DIM
    d_kv = N_KV_HEADS * HEAD_DIM
    pair_w = 2 * HEAD_DIM
    pairs = Q_PER_KV // 2
    first = pl.program_id(0) % tiles_per_batch == 0

    @pl.when(first)
    def _():
        kv_carry_ref[...] = jnp.zeros_like(kv_carry_ref)

    x = x_ref[...]
    qkv = _dot(x.astype(BF16), wqkv_ref[...]) + bqkv_ref[...]
    q = (qkv[:, :d_q] * (1.0 / math.sqrt(HEAD_DIM))).astype(BF16)
    kv = qkv[:, d_q:]
    kv_all = jnp.concatenate([kv_carry_ref[...], kv], axis=0)
    kv_carry_ref[...] = kv[tm - GROUP:]
    k_all = kv_all[:, :d_kv]
    v_all = kv_all[:, d_kv:]
    k_sw = pltpu.roll(k_all, HEAD_DIM, axis=1)
    v_sw = pltpu.roll(v_all, HEAD_DIM, axis=1)
    low = lax.broadcasted_iota(jnp.int32, k_all.shape, 1) < HEAD_DIM

    tq = _time_of_row(lax.broadcasted_iota(jnp.int32, (GROUP, 2 * GROUP), 0))
    col = lax.broadcasted_iota(jnp.int32, (GROUP, 2 * GROUP), 1)
    tk = _time_of_row(col & (GROUP - 1)) + (col & GROUP)
    diff = tq + GROUP - tk
    in_window = (diff >= 0) & (diff < GROUP)
    first_valid = in_window & (col >= jnp.where(first, GROUP, 0))
    nt = (((1,), (1,)), ((), ()))

    for kvh in range(N_KV_HEADS):
        k_src, k_alt = (k_all, k_sw) if kvh == 0 else (k_sw, k_all)
        v_src, v_alt = (v_all, v_sw) if kvh == 0 else (v_sw, v_all)
        k_even = jnp.where(low, k_src, 0.0).astype(BF16)
        k_odd = jnp.where(low, 0.0, k_alt).astype(BF16)
        v_even = jnp.where(low, v_src, 0.0).astype(BF16)
        v_odd = jnp.where(low, 0.0, v_alt).astype(BF16)
        for n in range(tm // GROUP):
            kr = slice(n * GROUP, (n + 2) * GROUP)
            qr = slice(n * GROUP, (n + 1) * GROUP)
            keys = jnp.concatenate([k_even[kr], k_odd[kr]], axis=0)
            vals = jnp.concatenate([v_even[kr], v_odd[kr]], axis=0)
            col0 = kvh * Q_PER_KV * HEAD_DIM
            qs = jnp.concatenate(
                [q[qr, col0 + p * pair_w:col0 + (p + 1) * pair_w] for p in range(pairs)], axis=0)
            s = lax.dot_general(qs, keys, nt, preferred_element_type=F32)
            valid = first_valid if n == 0 else in_window
            prob_rows = []
            for p in range(pairs):
                halves = []
                for e in range(2):
                    blk = s[p * GROUP:(p + 1) * GROUP, e * 2 * GROUP:(e + 1) * 2 * GROUP]
                    blk = jnp.where(valid, blk, -jnp.inf)
                    sink = sinks_ref[kvh * Q_PER_KV + 2 * p + e]
                    m = jnp.maximum(jnp.max(blk, axis=1, keepdims=True), sink)
                    pe = jnp.exp(blk - m)
                    den = jnp.sum(pe, axis=1, keepdims=True) + jnp.exp(sink - m)
                    halves.append((pe / den).astype(BF16))
                prob_rows.append(jnp.concatenate(halves, axis=1))
            probs = jnp.concatenate(prob_rows, axis=0)
            o = _dot(probs, vals)
            for p in range(pairs):
                attn_ref[qr, col0 + p * pair_w:col0 + (p + 1) * pair_w] = (
                    o[p * GROUP:(p + 1) * GROUP].astype(BF16))

    mix = _dot(attn_ref[...], wo_ref[...])
    o_ref[...] = _layer_norm(alpha * x + mix, g_ref[...], b_ref[...])


def _attn_call(x2d, w_qkv, b_qkv, sinks, w_o, ln_g, ln_b, *, alpha, seq):
    n, d = x2d.shape
    d_q = w_o.shape[0]
    tm = ROW_TILE
    kern = functools.partial(_attn_kernel, alpha=alpha, tiles_per_batch=seq // tm)
    return pl.pallas_call(
        kern,
        grid=(n // tm,),
        in_specs=[
            pl.BlockSpec(memory_space=pltpu.SMEM),
            pl.BlockSpec((tm, d), lambda i: (i, 0)),
            _const_spec(w_qkv.shape),
            _const_spec((1, w_qkv.shape[1])),
            _const_spec(w_o.shape),
            _const_spec((1, d)),
            _const_spec((1, d)),
        ],
        out_specs=pl.BlockSpec((tm, d), lambda i: (i, 0)),
        out_shape=jax.ShapeDtypeStruct((n, d), F32),
        scratch_shapes=[
            pltpu.VMEM((tm, d_q), BF16),
            pltpu.VMEM((GROUP, 2 * N_KV_HEADS * HEAD_DIM), F32),
        ],
        compiler_params=pltpu.CompilerParams(
            dimension_semantics=("arbitrary",), vmem_limit_bytes=VMEM_LIMIT_BYTES),
        name="swa_sink_attention",
    )(sinks, x2d, w_qkv.astype(BF16), b_qkv.reshape(1, -1), w_o.astype(BF16),
      ln_g.reshape(1, -1), ln_b.reshape(1, -1))


def kernel(x, ab_w_in, a_conv_w, a_conv_b, a_norm_g, a_norm_b, b_norm_g, b_norm_b, b_spatial_w, b_spatial_b, ab_w_out, c_w_qkv, c_b_qkv, c_sinks, c_w_o, ffn_w_up, ffn_conv_w, ffn_conv_b, ffn_w_down, ln_g, ln_b):
    bsz, seq, d = x.shape
    depth = ffn_w_up.shape[0]
    assert depth == 2 and seq % ROW_TILE == 0 and ROW_TILE % GROUP == 0
    alpha = float(np.float32((2.0 * depth) ** 0.25))
    h = x.reshape(bsz * seq, d)
    h = _mixer_call(h, ab_w_in[0], a_conv_w[0], a_conv_b[0], a_norm_g[0], a_norm_b[0],
                    b_norm_g[0], b_norm_b[0], b_spatial_w[0], b_spatial_b[0], ab_w_out[0],
                    ln_g[0, 0], ln_b[0, 0], alpha=alpha, seq=seq)
    h = _ffn_call(h, ffn_w_up[0], ffn_conv_w[0], ffn_conv_b[0], ffn_w_down[0],
                  ln_g[0, 1], ln_b[0, 1], alpha=alpha, seq=seq, natural_out=False)
    h = _attn_call(h, c_w_qkv[0], c_b_qkv[0], c_sinks[0], c_w_o[0],
                   ln_g[1, 0], ln_b[1, 0], alpha=alpha, seq=seq)
    h = _ffn_call(h, ffn_w_up[1], ffn_conv_w[1], ffn_conv_b[1], ffn_w_down[1],
                  ln_g[1, 1], ln_b[1, 1], alpha=alpha, seq=seq, natural_out=True)
    return h.reshape(bsz, seq, d)
```

```python
import functools
import math

import jax
import jax.numpy as jnp
import numpy as np
from jax import lax
from jax.experimental import pallas as pl
from jax.experimental.pallas import tpu as pltpu

F32 = jnp.float32
BF16 = jnp.bfloat16

LN_EPS = 1e-5
SUBLANES = 8
LANES = 128
GROUP = 128
VROWS = GROUP // SUBLANES
GMLP_GROUPS = 4
HEAD_DIM = 64
Q_PER_KV = 8
N_KV_HEADS = 2
VMEM_LIMIT_BYTES = 56 * 1024 * 1024

ROW_TILE = 512
FFN_ROW_TILE = 1024
FFN_SUB_TILE = 512
FFN_COL_TILE = 256

GELU_C0 = math.sqrt(2.0 / math.pi)
GELU_C1 = 0.044715 * GELU_C0


def _layer_norm(z, g, b):
    mu = jnp.mean(z, axis=-1, keepdims=True)
    d = z - mu
    var = jnp.mean(d * d, axis=-1, keepdims=True)
    return d * lax.rsqrt(var + LN_EPS) * g + b


def _dot(a, b):
    return jnp.dot(a, b, preferred_element_type=F32)


def _one_plus_tanh_gelu_arg(x):
    return 1.0 + jnp.tanh(x * (GELU_C0 + GELU_C1 * (x * x)))


def _sigmoid(x):
    return 0.5 * (1.0 + jnp.tanh(0.5 * x))


def _time_of_row(rho):
    return VROWS * (rho & (SUBLANES - 1)) + (rho >> 3)


def _perm_index():
    rho = np.arange(GROUP)
    return VROWS * (rho % SUBLANES) + rho // SUBLANES


def _sublane_shift(cur, prev):
    row = lax.broadcasted_iota(jnp.int32, cur.shape, 0)
    return jnp.where(row == 0, pltpu.roll(prev, 1, axis=0), pltpu.roll(cur, 1, axis=0))


def _const_spec(shape, lead=None):
    nd = len(shape)
    if lead is None:
        return pl.BlockSpec(shape, lambda i: (0,) * nd, pipeline_mode=pl.Buffered(1))
    return pl.BlockSpec((None,) + tuple(shape), lambda i: (lead,) + (0,) * nd,
                        pipeline_mode=pl.Buffered(1))


def _ffn_rows(rows, x_ref, wup_ref, cw_ref, cb_ref, wdn_ref, g, b, o_ref, act_ref, carry_ref,
              stage_ref, *, alpha, layer, d_ff, col_tile):
    tm = rows.stop - rows.start
    n_groups = tm // GROUP
    x = x_ref[rows, :]
    xb = x.astype(BF16)

    def conv_chunk(col, scale):
        h = _dot(xb, wup_ref[:, col:col + col_tile])
        w = cw_ref[:, col:col + col_tile]
        bias = cb_ref[layer:layer + 1, col:col + col_tile]
        if scale != 1.0:
            w, bias = w * scale, bias * scale
        prev = [carry_ref[0:SUBLANES, col:col + col_tile],
                carry_ref[SUBLANES:2 * SUBLANES, col:col + col_tile]]
        h1, h2 = [], []
        for gi in range(n_groups):
            hg = h[gi * GROUP:(gi + 1) * GROUP]
            cur = [hg[GROUP - 2 * SUBLANES:GROUP - SUBLANES], hg[GROUP - SUBLANES:]]
            wrapped = [_sublane_shift(cur[i], prev[i]) for i in range(2)]
            h1 += [wrapped[1], hg[:GROUP - SUBLANES]]
            h2 += [wrapped[0], wrapped[1], hg[:GROUP - 2 * SUBLANES]]
            prev = cur
        carry_ref[0:SUBLANES, col:col + col_tile] = prev[0]
        carry_ref[SUBLANES:2 * SUBLANES, col:col + col_tile] = prev[1]
        y = w[2:3] * h + bias
        y = y + w[1:2] * jnp.concatenate(h1, axis=0)
        y = y + w[0:1] * jnp.concatenate(h2, axis=0)
        return y

    for c in range(d_ff // col_tile):
        gate = conv_chunk(c * col_tile, 1.0)
        half_val = conv_chunk(d_ff + c * col_tile, 0.5)
        act = gate * _one_plus_tanh_gelu_arg(gate) * half_val
        act_ref[rows, c * col_tile:(c + 1) * col_tile] = act.astype(BF16)

    ffn = _dot(act_ref[rows, :], wdn_ref[...])
    y = _layer_norm(alpha * x + ffn, g, b)
    if stage_ref is None:
        o_ref[rows, :] = y
    else:
        d = y.shape[1]
        for c in range(d // LANES):
            for gi in range(n_groups):
                for j in range(VROWS):
                    r0 = gi * GROUP + j * SUBLANES
                    stage_ref[c, pl.ds(gi * GROUP + j, SUBLANES, stride=VROWS), :] = (
                        y[r0:r0 + SUBLANES, c * LANES:(c + 1) * LANES])
        for c in range(d // LANES):
            o_ref[rows, c * LANES:(c + 1) * LANES] = stage_ref[c]


def _ffn_kernel(x_ref, wup_ref, cw_ref, cb_ref, wdn_ref, g_ref, b_ref, o_ref,
                act_ref, carry_ref, *rest, tiles_per_batch, sub_tile, **static):
    tm = x_ref.shape[0]

    @pl.when(pl.program_id(0) % tiles_per_batch == 0)
    def _():
        carry_ref[...] = jnp.zeros_like(carry_ref)

    layer = static["layer"]
    g = g_ref[layer, 1:2, :]
    b = b_ref[layer, 1:2, :]
    stage_ref = rest[0] if rest else None
    for t in range(tm // sub_tile):
        _ffn_rows(slice(t * sub_tile, (t + 1) * sub_tile), x_ref, wup_ref, cw_ref, cb_ref, wdn_ref,
                  g, b, o_ref, act_ref, carry_ref, stage_ref, **static)


def _ffn_call(x2d, w_up_bf, conv_w, conv_b, w_down_bf, ln_g, ln_b, *, layer, alpha, seq,
              natural_out):
    n, d = x2d.shape
    d_ff = w_down_bf.shape[1]
    tm = FFN_ROW_TILE
    kern = functools.partial(_ffn_kernel, alpha=alpha, layer=layer, tiles_per_batch=seq // tm,
                             sub_tile=FFN_SUB_TILE, d_ff=d_ff, col_tile=FFN_COL_TILE)
    scratch = [pltpu.VMEM((tm, d_ff), BF16), pltpu.VMEM((2 * SUBLANES, 2 * d_ff), F32)]
    if natural_out:
        scratch.append(pltpu.VMEM((d // LANES, FFN_SUB_TILE, LANES), F32))
    return pl.pallas_call(
        kern,
        grid=(n // tm,),
        in_specs=[
            pl.BlockSpec((tm, d), lambda i: (i, 0)),
            _const_spec((d, 2 * d_ff), lead=layer),
            _const_spec((3, 2 * d_ff), lead=layer),
            _const_spec(conv_b.shape),
            _const_spec((d_ff, d), lead=layer),
            _const_spec(ln_g.shape),
            _const_spec(ln_b.shape),
        ],
        out_specs=pl.BlockSpec((tm, d), lambda i: (i, 0)),
        out_shape=jax.ShapeDtypeStruct((n, d), F32),
        scratch_shapes=scratch,
        compiler_params=pltpu.CompilerParams(
            dimension_semantics=("arbitrary",), vmem_limit_bytes=VMEM_LIMIT_BYTES),
        name="conv_ffn_out" if natural_out else "conv_ffn",
    )(x2d, w_up_bf, conv_w, conv_b, w_down_bf, ln_g, ln_b)


def _mixer_kernel(*refs, alpha, tiles_per_batch, width, n_col_blocks):
    x_refs = refs[:n_col_blocks]
    (win_ref, acw_ref, acb_ref, ang_ref, anb_ref, bng_ref, bnb_ref,
     ws_ref, sb_ref, wout_ref, g_ref, b_ref, o_ref, carry_ref) = refs[n_col_blocks:]
    tm = o_ref.shape[0]
    n_groups = tm // GROUP

    @pl.when(pl.program_id(0) % tiles_per_batch == 0)
    def _():
        carry_ref[...] = jnp.zeros_like(carry_ref)

    x = jnp.concatenate(
        [jnp.concatenate(
            [x_refs[c][pl.ds(g * GROUP + j, SUBLANES, stride=VROWS), :]
             for g in range(n_groups) for j in range(VROWS)], axis=0)
         for c in range(n_col_blocks)], axis=1)
    xb = x.astype(BF16)
    a_val = _dot(xb, win_ref[:, 0 * width:1 * width])
    a_gate = _dot(xb, win_ref[:, 1 * width:2 * width])
    b_u = _dot(xb, win_ref[:, 2 * width:3 * width])
    b_v = _dot(xb, win_ref[:, 3 * width:4 * width])

    a = a_val * _sigmoid(a_gate)
    ext = jnp.concatenate([carry_ref[...], a], axis=0)
    carry_ref[...] = a[tm - GROUP:]
    n_taps = acw_ref.shape[0]
    max_sub = (n_taps - 1) // VROWS + 1

    def sublane_shifted(z):
        out = []
        for g in range(n_groups + 1):
            zg = z[g * GROUP:(g + 1) * GROUP]
            zp = z[(g - 1) * GROUP:g * GROUP] if g > 0 else zg
            out += [_sublane_shift(zg[j * SUBLANES:(j + 1) * SUBLANES],
                                   zp[j * SUBLANES:(j + 1) * SUBLANES]) for j in range(VROWS)]
        return jnp.concatenate(out, axis=0)

    shifted = [ext]
    for _ in range(max_sub):
        shifted.append(sublane_shifted(shifted[-1]))
    acw = acw_ref[...]
    conv = jnp.broadcast_to(acb_ref[...], (tm, width))
    for k in range(n_taps):
        q, r = divmod(k, VROWS)
        lo = shifted[q][GROUP:]
        if r == 0:
            term = lo
        else:
            hi = shifted[q + 1][GROUP:]
            cut = (VROWS - r) * SUBLANES
            parts = []
            for g in range(n_groups):
                parts += [hi[g * GROUP + cut:(g + 1) * GROUP], lo[g * GROUP:g * GROUP + cut]]
            term = jnp.concatenate(parts, axis=0)
        j = n_taps - 1 - k
        conv = conv + acw[j:j + 1] * term
    a_out = _layer_norm(conv, ang_ref[...], anb_ref[...])
    a_out = a_out * _sigmoid(a_out)

    gelu_v = 0.5 * b_v * _one_plus_tanh_gelu_arg(b_v)
    v = _layer_norm(gelu_v, bng_ref[...], bnb_ref[...]).astype(BF16)
    ti = _time_of_row(lax.broadcasted_iota(jnp.int32, (GROUP, GROUP), 0))
    tj = _time_of_row(lax.broadcasted_iota(jnp.int32, (GROUP, GROUP), 1))
    gd = width // GMLP_GROUPS
    ws = [(jnp.where(ti >= tj, ws_ref[g], 0.0) * 0.5).astype(BF16) for g in range(GMLP_GROUPS)]
    half_bias = sb_ref[...] * 0.5
    chunks = []
    for c in range(n_groups):
        rows = slice(c * GROUP, (c + 1) * GROUP)
        cols = [_dot(ws[g], v[rows, g * gd:(g + 1) * gd]) for g in range(GMLP_GROUPS)]
        chunks.append(jnp.concatenate(cols, axis=1) + half_bias)
    b_out = b_u * _one_plus_tanh_gelu_arg(b_u) * jnp.concatenate(chunks, axis=0)

    mix = _dot(jnp.concatenate([a_out.astype(BF16), b_out.astype(BF16)], axis=1), wout_ref[...])
    o_ref[...] = _layer_norm(alpha * x + mix, g_ref[0, 0:1, :], b_ref[0, 0:1, :])


def _mixer_call(x2d, w_in_bf, a_conv_w, a_conv_b, a_norm_g, a_norm_b, b_norm_g, b_norm_b,
                b_spatial_w, b_spatial_b, w_out_bf, ln_g, ln_b, *, alpha, seq):
    n, d = x2d.shape
    width = a_conv_w.shape[2]
    gd = width // GMLP_GROUPS
    tm = ROW_TILE
    n_col_blocks = d // LANES
    perm = _perm_index()
    ws_perm = b_spatial_w[0][:, perm][:, :, perm]
    sbias = jnp.repeat(b_spatial_b[0].T[perm], gd, axis=1)
    kern = functools.partial(_mixer_kernel, alpha=alpha, tiles_per_batch=seq // tm, width=width,
                             n_col_blocks=n_col_blocks)
    x_specs = [pl.BlockSpec((tm, LANES), functools.partial(lambda c, i: (i, c), c))
               for c in range(n_col_blocks)]
    return pl.pallas_call(
        kern,
        grid=(n // tm,),
        in_specs=x_specs + [
            _const_spec(w_in_bf.shape[1:], lead=0),
            _const_spec(a_conv_w.shape[1:], lead=0),
            _const_spec(a_conv_b.shape),
            _const_spec(a_norm_g.shape),
            _const_spec(a_norm_b.shape),
            _const_spec(b_norm_g.shape),
            _const_spec(b_norm_b.shape),
            _const_spec(ws_perm.shape),
            _const_spec(sbias.shape),
            _const_spec(w_out_bf.shape[1:], lead=0),
            _const_spec(ln_g.shape),
            _const_spec(ln_b.shape),
        ],
        out_specs=pl.BlockSpec((tm, d), lambda i: (i, 0)),
        out_shape=jax.ShapeDtypeStruct((n, d), F32),
        scratch_shapes=[pltpu.VMEM((GROUP, width), F32)],
        compiler_params=pltpu.CompilerParams(
            dimension_semantics=("arbitrary",), vmem_limit_bytes=VMEM_LIMIT_BYTES),
        name="conv_gmlp_mixer",
    )(*([x2d] * n_col_blocks), w_in_bf, a_conv_w, a_conv_b, a_norm_g, a_norm_b, b_norm_g, b_norm_b,
      ws_perm, sbias, w_out_bf, ln_g, ln_b)


def _attn_kernel(sinks_ref, x_ref, wqkv_ref, bqkv_ref, wo_ref, g_ref, b_ref, o_ref,
                 attn_ref, kv_carry_ref, *, alpha, tiles_per_batch):
    tm = x_ref.shape[0]
    d_q = N_KV_HEADS * Q_PER_KV * HEAD_DIM
    d_kv = N_KV_HEADS * HEAD_DIM
    pair_w = 2 * HEAD_DIM
    pairs = Q_PER_KV // 2
    first = pl.program_id(0) % tiles_per_batch == 0

    @pl.when(first)
    def _():
        kv_carry_ref[...] = jnp.zeros_like(kv_carry_ref)

    x = x_ref[...]
    qkv = _dot(x.astype(BF16), wqkv_ref[...]) + bqkv_ref[...]
    q = (qkv[:, :d_q] * (1.0 / math.sqrt(HEAD_DIM))).astype(BF16)
    kv = qkv[:, d_q:]
    kv_all = jnp.concatenate([kv_carry_ref[...], kv], axis=0)
    kv_carry_ref[...] = kv[tm - GROUP:]
    k_all = kv_all[:, :d_kv]
    v_all = kv_all[:, d_kv:]
    k_sw = pltpu.roll(k_all, HEAD_DIM, axis=1)
    v_sw = pltpu.roll(v_all, HEAD_DIM, axis=1)
    low = lax.broadcasted_iota(jnp.int32, k_all.shape, 1) < HEAD_DIM
    low_q = lax.broadcasted_iota(jnp.int32, (GROUP, pair_w), 1) < HEAD_DIM

    tq = _time_of_row(lax.broadcasted_iota(jnp.int32, (GROUP, GROUP), 0))
    tk = _time_of_row(lax.broadcasted_iota(jnp.int32, (GROUP, GROUP), 1))
    use_cur = tk <= tq
    prev_fill = jnp.where(first, -jnp.inf, 0.0).astype(F32)
    nt = (((1,), (1,)), ((), ()))

    for kvh in range(N_KV_HEADS):
        k_src, k_alt = (k_all, k_sw) if kvh == 0 else (k_sw, k_all)
        v_src, v_alt = (v_all, v_sw) if kvh == 0 else (v_sw, v_all)
        k_even = jnp.where(low, k_src, 0.0).astype(BF16)
        k_odd = jnp.where(low, 0.0, k_alt).astype(BF16)
        v_even = jnp.where(low, v_src, 0.0).astype(BF16)
        v_odd = jnp.where(low, 0.0, v_alt).astype(BF16)
        for n in range(tm // GROUP):
            kr = slice(n * GROUP, (n + 2) * GROUP)
            qr = slice(n * GROUP, (n + 1) * GROUP)
            keys = jnp.concatenate([k_even[kr], k_odd[kr]], axis=0)
            vals = jnp.concatenate([v_even[kr], v_odd[kr]], axis=0)
            col0 = kvh * Q_PER_KV * HEAD_DIM
            qs = jnp.concatenate(
                [q[qr, col0 + p * pair_w:col0 + (p + 1) * pair_w] for p in range(pairs)], axis=0)
            s = lax.dot_general(qs, keys, nt, preferred_element_type=F32)
            prob_rows, inv_den = [], []
            for p in range(pairs):
                halves, inv = [], []
                for e in range(2):
                    c0 = e * 2 * GROUP
                    s_prev = s[p * GROUP:(p + 1) * GROUP, c0:c0 + GROUP]
                    s_cur = s[p * GROUP:(p + 1) * GROUP, c0 + GROUP:c0 + 2 * GROUP]
                    if n == 0:
                        s_prev = s_prev + prev_fill
                    merged = jnp.where(use_cur, s_cur, s_prev)
                    sink = sinks_ref[0, kvh * Q_PER_KV + 2 * p + e]
                    m = jnp.maximum(jnp.max(merged, axis=1, keepdims=True), sink)
                    pe = jnp.exp(merged - m)
                    den = jnp.sum(pe, axis=1, keepdims=True) + jnp.exp(sink - m)
                    inv.append(1.0 / den)
                    halves += [jnp.where(use_cur, 0.0, pe).astype(BF16),
                               jnp.where(use_cur, pe, 0.0).astype(BF16)]
                prob_rows.append(jnp.concatenate(halves, axis=1))
                inv_den.append(jnp.where(low_q, inv[0], inv[1]))
            probs = jnp.concatenate(prob_rows, axis=0)
            o = _dot(probs, vals)
            for p in range(pairs):
                attn_ref[qr, col0 + p * pair_w:col0 + (p + 1) * pair_w] = (
                    o[p * GROUP:(p + 1) * GROUP] * inv_den[p]).astype(BF16)

    mix = _dot(attn_ref[...], wo_ref[...])
    o_ref[...] = _layer_norm(alpha * x + mix, g_ref[1, 0:1, :], b_ref[1, 0:1, :])


def _attn_call(x2d, w_qkv_bf, b_qkv, sinks, w_o_bf, ln_g, ln_b, *, alpha, seq):
    n, d = x2d.shape
    d_q = w_o_bf.shape[1]
    tm = ROW_TILE
    kern = functools.partial(_attn_kernel, alpha=alpha, tiles_per_batch=seq // tm)
    return pl.pallas_call(
        kern,
        grid=(n // tm,),
        in_specs=[
            pl.BlockSpec(memory_space=pltpu.SMEM),
            pl.BlockSpec((tm, d), lambda i: (i, 0)),
            _const_spec(w_qkv_bf.shape[1:], lead=0),
            _const_spec(b_qkv.shape),
            _const_spec(w_o_bf.shape[1:], lead=0),
            _const_spec(ln_g.shape),
            _const_spec(ln_b.shape),
        ],
        out_specs=pl.BlockSpec((tm, d), lambda i: (i, 0)),
        out_shape=jax.ShapeDtypeStruct((n, d), F32),
        scratch_shapes=[
            pltpu.VMEM((tm, d_q), BF16),
            pltpu.VMEM((GROUP, 2 * N_KV_HEADS * HEAD_DIM), F32),
        ],
        compiler_params=pltpu.CompilerParams(
            dimension_semantics=("arbitrary",), vmem_limit_bytes=VMEM_LIMIT_BYTES),
        name="swa_sink_attention",
    )(sinks, x2d, w_qkv_bf, b_qkv, w_o_bf, ln_g, ln_b)


def kernel(x, ab_w_in, a_conv_w, a_conv_b, a_norm_g, a_norm_b, b_norm_g, b_norm_b, b_spatial_w, b_spatial_b, ab_w_out, c_w_qkv, c_b_qkv, c_sinks, c_w_o, ffn_w_up, ffn_conv_w, ffn_conv_b, ffn_w_down, ln_g, ln_b):
    bsz, seq, d = x.shape
    depth = ffn_w_up.shape[0]
    assert depth == 2 and seq % FFN_ROW_TILE == 0 and seq % ROW_TILE == 0
    alpha = float(np.float32((2.0 * depth) ** 0.25))
    w_up_bf, w_down_bf = ffn_w_up.astype(BF16), ffn_w_down.astype(BF16)
    h = x.reshape(bsz * seq, d)
    h = _mixer_call(h, ab_w_in.astype(BF16), a_conv_w, a_conv_b, a_norm_g, a_norm_b,
                    b_norm_g, b_norm_b, b_spatial_w, b_spatial_b, ab_w_out.astype(BF16),
                    ln_g, ln_b, alpha=alpha, seq=seq)
    h = _ffn_call(h, w_up_bf, ffn_conv_w, ffn_conv_b, w_down_bf, ln_g, ln_b,
                  layer=0, alpha=alpha, seq=seq, natural_out=False)
    h = _attn_call(h, c_w_qkv.astype(BF16), c_b_qkv, c_sinks, c_w_o.astype(BF16),
                   ln_g, ln_b, alpha=alpha, seq=seq)
    h = _ffn_call(h, w_up_bf, ffn_conv_w, ffn_conv_b, w_down_bf, ln_g, ln_b,
                  layer=1, alpha=alpha, seq=seq, natural_out=True)
    return h.reshape(bsz, seq, d)
```

```python
import functools
import math

import jax
import jax.numpy as jnp
import numpy as np
from jax import lax
from jax.experimental import pallas as pl
from jax.experimental.pallas import tpu as pltpu

F32 = jnp.float32
BF16 = jnp.bfloat16

LN_EPS = 1e-5
SUBLANES = 8
LANES = 128
GROUP = 128
VROWS = GROUP // SUBLANES
GMLP_GROUPS = 4
HEAD_DIM = 64
Q_PER_KV = 8
N_KV_HEADS = 2
VMEM_LIMIT_BYTES = 56 * 1024 * 1024

ROW_TILE = 512
FFN_ROW_TILE = 1024
FFN_SUB_TILE = 512
FFN_COL_TILE = 256

GELU_C0 = math.sqrt(2.0 / math.pi)
GELU_C1 = 0.044715 * GELU_C0


def _layer_norm(z, g, b):
    mu = jnp.mean(z, axis=-1, keepdims=True)
    d = z - mu
    var = jnp.mean(d * d, axis=-1, keepdims=True)
    return d * lax.rsqrt(var + LN_EPS) * g + b


def _dot(a, b):
    return jnp.dot(a, b, preferred_element_type=F32)


def _one_plus_tanh_gelu_arg(x):
    return 1.0 + jnp.tanh(x * (GELU_C0 + GELU_C1 * (x * x)))


def _sigmoid(x):
    return 0.5 * (1.0 + jnp.tanh(0.5 * x))


def _time_of_row(rho):
    return VROWS * (rho & (SUBLANES - 1)) + (rho >> 3)


def _perm_index():
    rho = np.arange(GROUP)
    return VROWS * (rho % SUBLANES) + rho // SUBLANES


def _sublane_shift(cur, prev):
    row = lax.broadcasted_iota(jnp.int32, cur.shape, 0)
    return jnp.where(row == 0, pltpu.roll(prev, 1, axis=0), pltpu.roll(cur, 1, axis=0))


def _const_spec(shape, lead=None):
    nd = len(shape)
    if lead is None:
        return pl.BlockSpec(shape, lambda i: (0,) * nd, pipeline_mode=pl.Buffered(1))
    return pl.BlockSpec((None,) + tuple(shape), lambda i: (lead,) + (0,) * nd,
                        pipeline_mode=pl.Buffered(1))


def _ffn_rows(rows, x_ref, wup_ref, cw_ref, cb_ref, wdn_ref, g, b, o_ref, act_ref, carry_ref,
              stage_ref, *, alpha, layer, d_ff, col_tile):
    tm = rows.stop - rows.start
    n_groups = tm // GROUP
    x = x_ref[rows, :]
    xb = x.astype(BF16)

    def conv_chunk(col, scale):
        h = _dot(xb, wup_ref[:, col:col + col_tile])
        w = cw_ref[:, col:col + col_tile]
        bias = cb_ref[layer:layer + 1, col:col + col_tile]
        if scale != 1.0:
            w, bias = w * scale, bias * scale
        prev = [carry_ref[0:SUBLANES, col:col + col_tile],
                carry_ref[SUBLANES:2 * SUBLANES, col:col + col_tile]]
        h1, h2 = [], []
        for gi in range(n_groups):
            hg = h[gi * GROUP:(gi + 1) * GROUP]
            cur = [hg[GROUP - 2 * SUBLANES:GROUP - SUBLANES], hg[GROUP - SUBLANES:]]
            wrapped = [_sublane_shift(cur[i], prev[i]) for i in range(2)]
            h1 += [wrapped[1], hg[:GROUP - SUBLANES]]
            h2 += [wrapped[0], wrapped[1], hg[:GROUP - 2 * SUBLANES]]
            prev = cur
        carry_ref[0:SUBLANES, col:col + col_tile] = prev[0]
        carry_ref[SUBLANES:2 * SUBLANES, col:col + col_tile] = prev[1]
        y = w[2:3] * h + bias
        y = y + w[1:2] * jnp.concatenate(h1, axis=0)
        y = y + w[0:1] * jnp.concatenate(h2, axis=0)
        return y

    for c in range(d_ff // col_tile):
        gate = conv_chunk(c * col_tile, 1.0)
        half_val = conv_chunk(d_ff + c * col_tile, 0.5)
        act = gate * _one_plus_tanh_gelu_arg(gate) * half_val
        act_ref[rows, c * col_tile:(c + 1) * col_tile] = act.astype(BF16)

    ffn = _dot(act_ref[rows, :], wdn_ref[...])
    y = _layer_norm(alpha * x + ffn, g, b)
    if stage_ref is None:
        o_ref[rows, :] = y
    else:
        d = y.shape[1]
        for c in range(d // LANES):
            stage_ref[c] = y[:, c * LANES:(c + 1) * LANES]
        for c in range(d // LANES):
            for gi in range(n_groups):
                for s in range(SUBLANES):
                    for half in range(VROWS // SUBLANES):
                        src = gi * GROUP + half * SUBLANES * SUBLANES + s
                        dst = rows.start + gi * GROUP + VROWS * s + SUBLANES * half
                        o_ref[dst:dst + SUBLANES, c * LANES:(c + 1) * LANES] = (
                            stage_ref[c, pl.ds(src, SUBLANES, stride=SUBLANES), :])


def _ffn_kernel(x_ref, wup_ref, cw_ref, cb_ref, wdn_ref, g_ref, b_ref, o_ref,
                act_ref, carry_ref, *rest, tiles_per_batch, sub_tile, **static):
    tm = x_ref.shape[0]

    @pl.when(pl.program_id(0) % tiles_per_batch == 0)
    def _():
        carry_ref[...] = jnp.zeros_like(carry_ref)

    layer = static["layer"]
    g = g_ref[layer, 1:2, :]
    b = b_ref[layer, 1:2, :]
    stage_ref = rest[0] if rest else None
    for t in range(tm // sub_tile):
        _ffn_rows(slice(t * sub_tile, (t + 1) * sub_tile), x_ref, wup_ref, cw_ref, cb_ref, wdn_ref,
                  g, b, o_ref, act_ref, carry_ref, stage_ref, **static)


def _ffn_call(x2d, w_up_bf, conv_w, conv_b, w_down_bf, ln_g, ln_b, *, layer, alpha, seq,
              natural_out):
    n, d = x2d.shape
    d_ff = w_down_bf.shape[1]
    tm = FFN_ROW_TILE
    kern = functools.partial(_ffn_kernel, alpha=alpha, layer=layer, tiles_per_batch=seq // tm,
                             sub_tile=FFN_SUB_TILE, d_ff=d_ff, col_tile=FFN_COL_TILE)
    scratch = [pltpu.VMEM((tm, d_ff), BF16), pltpu.VMEM((2 * SUBLANES, 2 * d_ff), F32)]
    if natural_out:
        scratch.append(pltpu.VMEM((d // LANES, FFN_SUB_TILE, LANES), F32))
    return pl.pallas_call(
        kern,
        grid=(n // tm,),
        in_specs=[
            pl.BlockSpec((tm, d), lambda i: (i, 0)),
            _const_spec((d, 2 * d_ff), lead=layer),
            _const_spec((3, 2 * d_ff), lead=layer),
            _const_spec(conv_b.shape),
            _const_spec((d_ff, d), lead=layer),
            _const_spec(ln_g.shape),
            _const_spec(ln_b.shape),
        ],
        out_specs=pl.BlockSpec((tm, d), lambda i: (i, 0)),
        out_shape=jax.ShapeDtypeStruct((n, d), F32),
        scratch_shapes=scratch,
        compiler_params=pltpu.CompilerParams(
            dimension_semantics=("arbitrary",), vmem_limit_bytes=VMEM_LIMIT_BYTES),
        name="conv_ffn_out" if natural_out else "conv_ffn",
    )(x2d, w_up_bf, conv_w, conv_b, w_down_bf, ln_g, ln_b)


def _mixer_kernel(*refs, alpha, tiles_per_batch, width, n_col_blocks):
    x_refs = refs[:n_col_blocks]
    (win_f32_ref, acw_ref, acb_ref, ang_ref, anb_ref, bng_ref, bnb_ref,
     ws_ref, sb_ref, wout_f32_ref, g_ref, b_ref, o_ref, carry_ref, win_ref, wout_ref) = (
         refs[n_col_blocks:])
    tm = o_ref.shape[0]
    n_groups = tm // GROUP

    @pl.when(pl.program_id(0) == 0)
    def _():
        win_ref[...] = win_f32_ref[...].astype(BF16)
        wout_ref[...] = wout_f32_ref[...].astype(BF16)

    @pl.when(pl.program_id(0) % tiles_per_batch == 0)
    def _():
        carry_ref[...] = jnp.zeros_like(carry_ref)

    x = jnp.concatenate(
        [jnp.concatenate(
            [x_refs[c][pl.ds(g * GROUP + j, SUBLANES, stride=VROWS), :]
             for g in range(n_groups) for j in range(VROWS)], axis=0)
         for c in range(n_col_blocks)], axis=1)
    xb = x.astype(BF16)
    a_val = _dot(xb, win_ref[:, 0 * width:1 * width])
    a_gate = _dot(xb, win_ref[:, 1 * width:2 * width])
    b_u = _dot(xb, win_ref[:, 2 * width:3 * width])
    b_v = _dot(xb, win_ref[:, 3 * width:4 * width])

    a = a_val * _sigmoid(a_gate)
    ext = jnp.concatenate([carry_ref[...], a], axis=0)
    carry_ref[...] = a[tm - GROUP:]
    n_taps = acw_ref.shape[0]
    max_sub = (n_taps - 1) // VROWS + 1

    def sublane_shifted(z):
        out = []
        for g in range(n_groups + 1):
            zg = z[g * GROUP:(g + 1) * GROUP]
            zp = z[(g - 1) * GROUP:g * GROUP] if g > 0 else zg
            out += [_sublane_shift(zg[j * SUBLANES:(j + 1) * SUBLANES],
                                   zp[j * SUBLANES:(j + 1) * SUBLANES]) for j in range(VROWS)]
        return jnp.concatenate(out, axis=0)

    shifted = [ext]
    for _ in range(max_sub):
        shifted.append(sublane_shifted(shifted[-1]))
    acw = acw_ref[...]
    conv = jnp.broadcast_to(acb_ref[...], (tm, width))
    for k in range(n_taps):
        q, r = divmod(k, VROWS)
        lo = shifted[q][GROUP:]
        if r == 0:
            term = lo
        else:
            hi = shifted[q + 1][GROUP:]
            cut = (VROWS - r) * SUBLANES
            parts = []
            for g in range(n_groups):
                parts += [hi[g * GROUP + cut:(g + 1) * GROUP], lo[g * GROUP:g * GROUP + cut]]
            term = jnp.concatenate(parts, axis=0)
        j = n_taps - 1 - k
        conv = conv + acw[j:j + 1] * term
    a_out = _layer_norm(conv, ang_ref[...], anb_ref[...])
    a_out = a_out * _sigmoid(a_out)

    gelu_v = 0.5 * b_v * _one_plus_tanh_gelu_arg(b_v)
    v = _layer_norm(gelu_v, bng_ref[...], bnb_ref[...]).astype(BF16)
    ti = _time_of_row(lax.broadcasted_iota(jnp.int32, (GROUP, GROUP), 0))
    tj = _time_of_row(lax.broadcasted_iota(jnp.int32, (GROUP, GROUP), 1))
    gd = width // GMLP_GROUPS
    ws = [(jnp.where(ti >= tj, ws_ref[g], 0.0) * 0.5).astype(BF16) for g in range(GMLP_GROUPS)]
    half_bias = sb_ref[...] * 0.5
    chunks = []
    for c in range(n_groups):
        rows = slice(c * GROUP, (c + 1) * GROUP)
        cols = [_dot(ws[g], v[rows, g * gd:(g + 1) * gd]) for g in range(GMLP_GROUPS)]
        chunks.append(jnp.concatenate(cols, axis=1) + half_bias)
    b_out = b_u * _one_plus_tanh_gelu_arg(b_u) * jnp.concatenate(chunks, axis=0)

    mix = _dot(jnp.concatenate([a_out.astype(BF16), b_out.astype(BF16)], axis=1), wout_ref[...])
    o_ref[...] = _layer_norm(alpha * x + mix, g_ref[0, 0:1, :], b_ref[0, 0:1, :])


def _mixer_call(x2d, w_in, a_conv_w, a_conv_b, a_norm_g, a_norm_b, b_norm_g, b_norm_b,
                b_spatial_w, b_spatial_b, w_out, ln_g, ln_b, *, alpha, seq):
    n, d = x2d.shape
    width = a_conv_w.shape[2]
    gd = width // GMLP_GROUPS
    tm = ROW_TILE
    n_col_blocks = d // LANES
    perm = _perm_index()
    ws_perm = b_spatial_w[0][:, perm][:, :, perm]
    sbias = jnp.repeat(b_spatial_b[0].T[perm], gd, axis=1)
    kern = functools.partial(_mixer_kernel, alpha=alpha, tiles_per_batch=seq // tm, width=width,
                             n_col_blocks=n_col_blocks)
    x_specs = [pl.BlockSpec((tm, LANES), functools.partial(lambda c, i: (i, c), c))
               for c in range(n_col_blocks)]
    return pl.pallas_call(
        kern,
        grid=(n // tm,),
        in_specs=x_specs + [
            _const_spec(w_in.shape[1:], lead=0),
            _const_spec(a_conv_w.shape[1:], lead=0),
            _const_spec(a_conv_b.shape),
            _const_spec(a_norm_g.shape),
            _const_spec(a_norm_b.shape),
            _const_spec(b_norm_g.shape),
            _const_spec(b_norm_b.shape),
            _const_spec(ws_perm.shape),
            _const_spec(sbias.shape),
            _const_spec(w_out.shape[1:], lead=0),
            _const_spec(ln_g.shape),
            _const_spec(ln_b.shape),
        ],
        out_specs=pl.BlockSpec((tm, d), lambda i: (i, 0)),
        out_shape=jax.ShapeDtypeStruct((n, d), F32),
        scratch_shapes=[pltpu.VMEM((GROUP, width), F32),
                        pltpu.VMEM(w_in.shape[1:], BF16),
                        pltpu.VMEM(w_out.shape[1:], BF16)],
        compiler_params=pltpu.CompilerParams(
            dimension_semantics=("arbitrary",), vmem_limit_bytes=VMEM_LIMIT_BYTES),
        name="conv_gmlp_mixer",
    )(*([x2d] * n_col_blocks), w_in, a_conv_w, a_conv_b, a_norm_g, a_norm_b, b_norm_g, b_norm_b,
      ws_perm, sbias, w_out, ln_g, ln_b)


def _attn_kernel(sinks_ref, x_ref, wqkv_f32_ref, bqkv_ref, wo_f32_ref, g_ref, b_ref, o_ref,
                 attn_ref, kv_carry_ref, wqkv_ref, wo_ref, *, alpha, tiles_per_batch):
    tm = x_ref.shape[0]
    d_q = N_KV_HEADS * Q_PER_KV * HEAD_DIM
    d_kv = N_KV_HEADS * HEAD_DIM
    pair_w = 2 * HEAD_DIM
    pairs = Q_PER_KV // 2
    first = pl.program_id(0) % tiles_per_batch == 0

    @pl.when(first)
    def _():
        kv_carry_ref[...] = jnp.zeros_like(kv_carry_ref)

    @pl.when(pl.program_id(0) == 0)
    def _():
        wqkv_ref[...] = wqkv_f32_ref[...].astype(BF16)
        wo_ref[...] = wo_f32_ref[...].astype(BF16)

    x = x_ref[...]
    qkv = _dot(x.astype(BF16), wqkv_ref[...]) + bqkv_ref[...]
    q = (qkv[:, :d_q] * (1.0 / math.sqrt(HEAD_DIM))).astype(BF16)
    kv = qkv[:, d_q:]
    kv_all = jnp.concatenate([kv_carry_ref[...], kv], axis=0)
    kv_carry_ref[...] = kv[tm - GROUP:]
    k_all = kv_all[:, :d_kv]
    v_all = kv_all[:, d_kv:]
    k_sw = pltpu.roll(k_all, HEAD_DIM, axis=1)
    v_sw = pltpu.roll(v_all, HEAD_DIM, axis=1)
    low = lax.broadcasted_iota(jnp.int32, k_all.shape, 1) < HEAD_DIM
    low_q = lax.broadcasted_iota(jnp.int32, (GROUP, pair_w), 1) < HEAD_DIM

    tq = _time_of_row(lax.broadcasted_iota(jnp.int32, (GROUP, GROUP), 0))
    tk = _time_of_row(lax.broadcasted_iota(jnp.int32, (GROUP, GROUP), 1))
    use_cur = tk <= tq
    prev_fill = jnp.where(first, -jnp.inf, 0.0).astype(F32)
    nt = (((1,), (1,)), ((), ()))

    for kvh in range(N_KV_HEADS):
        k_src, k_alt = (k_all, k_sw) if kvh == 0 else (k_sw, k_all)
        v_src, v_alt = (v_all, v_sw) if kvh == 0 else (v_sw, v_all)
        k_even = jnp.where(low, k_src, 0.0).astype(BF16)
        k_odd = jnp.where(low, 0.0, k_alt).astype(BF16)
        v_even = jnp.where(low, v_src, 0.0).astype(BF16)
        v_odd = jnp.where(low, 0.0, v_alt).astype(BF16)
        for n in range(tm // GROUP):
            kr = slice(n * GROUP, (n + 2) * GROUP)
            qr = slice(n * GROUP, (n + 1) * GROUP)
            keys = jnp.concatenate([k_even[kr], k_odd[kr]], axis=0)
            vals = jnp.concatenate([v_even[kr], v_odd[kr]], axis=0)
            col0 = kvh * Q_PER_KV * HEAD_DIM
            qs = jnp.concatenate(
                [q[qr, col0 + p * pair_w:col0 + (p + 1) * pair_w] for p in range(pairs)], axis=0)
            s = lax.dot_general(qs, keys, nt, preferred_element_type=F32)
            prob_rows, inv_den = [], []
            for p in range(pairs):
                halves, inv = [], []
                for e in range(2):
                    c0 = e * 2 * GROUP
                    s_prev = s[p * GROUP:(p + 1) * GROUP, c0:c0 + GROUP]
                    s_cur = s[p * GROUP:(p + 1) * GROUP, c0 + GROUP:c0 + 2 * GROUP]
                    if n == 0:
                        s_prev = s_prev + prev_fill
                    merged = jnp.where(use_cur, s_cur, s_prev)
                    sink = sinks_ref[0, kvh * Q_PER_KV + 2 * p + e]
                    m = jnp.maximum(jnp.max(merged, axis=1, keepdims=True), sink)
                    pe = jnp.exp(merged - m)
                    den = jnp.sum(pe, axis=1, keepdims=True) + jnp.exp(sink - m)
                    inv.append(1.0 / den)
                    halves += [jnp.where(use_cur, 0.0, pe).astype(BF16),
                               jnp.where(use_cur, pe, 0.0).astype(BF16)]
                prob_rows.append(jnp.concatenate(halves, axis=1))
                inv_den.append(jnp.where(low_q, inv[0], inv[1]))
            probs = jnp.concatenate(prob_rows, axis=0)
            o = _dot(probs, vals)
            for p in range(pairs):
                attn_ref[qr, col0 + p * pair_w:col0 + (p + 1) * pair_w] = (
                    o[p * GROUP:(p + 1) * GROUP] * inv_den[p]).astype(BF16)

    mix = _dot(attn_ref[...], wo_ref[...])
    o_ref[...] = _layer_norm(alpha * x + mix, g_ref[1, 0:1, :], b_ref[1, 0:1, :])


def _attn_call(x2d, w_qkv, b_qkv, sinks, w_o, ln_g, ln_b, *, alpha, seq):
    n, d = x2d.shape
    d_q = w_o.shape[1]
    tm = ROW_TILE
    kern = functools.partial(_attn_kernel, alpha=alpha, tiles_per_batch=seq // tm)
    return pl.pallas_call(
        kern,
        grid=(n // tm,),
        in_specs=[
            pl.BlockSpec(memory_space=pltpu.SMEM),
            pl.BlockSpec((tm, d), lambda i: (i, 0)),
            _const_spec(w_qkv.shape[1:], lead=0),
            _const_spec(b_qkv.shape),
            _const_spec(w_o.shape[1:], lead=0),
            _const_spec(ln_g.shape),
            _const_spec(ln_b.shape),
        ],
        out_specs=pl.BlockSpec((tm, d), lambda i: (i, 0)),
        out_shape=jax.ShapeDtypeStruct((n, d), F32),
        scratch_shapes=[
            pltpu.VMEM((tm, d_q), BF16),
            pltpu.VMEM((GROUP, 2 * N_KV_HEADS * HEAD_DIM), F32),
            pltpu.VMEM(w_qkv.shape[1:], BF16),
            pltpu.VMEM(w_o.shape[1:], BF16),
        ],
        compiler_params=pltpu.CompilerParams(
            dimension_semantics=("arbitrary",), vmem_limit_bytes=VMEM_LIMIT_BYTES),
        name="swa_sink_attention",
    )(sinks, x2d, w_qkv, b_qkv, w_o, ln_g, ln_b)


def kernel(x, ab_w_in, a_conv_w, a_conv_b, a_norm_g, a_norm_b, b_norm_g, b_norm_b, b_spatial_w, b_spatial_b, ab_w_out, c_w_qkv, c_b_qkv, c_sinks, c_w_o, ffn_w_up, ffn_conv_w, ffn_conv_b, ffn_w_down, ln_g, ln_b):
    bsz, seq, d = x.shape
    depth = ffn_w_up.shape[0]
    assert depth == 2 and seq % FFN_ROW_TILE == 0 and seq % ROW_TILE == 0
    alpha = float(np.float32((2.0 * depth) ** 0.25))
    w_up_bf, w_down_bf = ffn_w_up.astype(BF16), ffn_w_down.astype(BF16)
    h = x.reshape(bsz * seq, d)
    h = _mixer_call(h, ab_w_in, a_conv_w, a_conv_b, a_norm_g, a_norm_b,
                    b_norm_g, b_norm_b, b_spatial_w, b_spatial_b, ab_w_out,
                    ln_g, ln_b, alpha=alpha, seq=seq)
    h = _ffn_call(h, w_up_bf, ffn_conv_w, ffn_conv_b, w_down_bf, ln_g, ln_b,
                  layer=0, alpha=alpha, seq=seq, natural_out=False)
    h = _attn_call(h, c_w_qkv, c_b_qkv, c_sinks, c_w_o, ln_g, ln_b, alpha=alpha, seq=seq)
    h = _ffn_call(h, w_up_bf, ffn_conv_w, ffn_conv_b, w_down_bf, ln_g, ln_b,
                  layer=1, alpha=alpha, seq=seq, natural_out=True)
    return h.reshape(bsz, seq, d)
```

```python
import functools
import math

import jax
import jax.numpy as jnp
import numpy as np
from jax import lax
from jax.experimental import pallas as pl
from jax.experimental.pallas import tpu as pltpu

F32 = jnp.float32
BF16 = jnp.bfloat16

LN_EPS = 1e-5
SUBLANES = 8
LANES = 128
BF16_ROWS = 16
GROUP = 128
VROWS = GROUP // SUBLANES
GMLP_GROUPS = 4
HEAD_DIM = 64
Q_PER_KV = 8
N_KV_HEADS = 2
VMEM_LIMIT_BYTES = 56 * 1024 * 1024

ROW_TILE = 512
FFN_ROW_TILE = 1024
FFN_SUB_TILE = 512
FFN_COL_TILE = 256

GELU_C0 = math.sqrt(2.0 / math.pi)
GELU_C1 = 0.044715 * GELU_C0


def _layer_norm(z, g, b):
    mu = jnp.mean(z, axis=-1, keepdims=True)
    d = z - mu
    var = jnp.mean(d * d, axis=-1, keepdims=True)
    return d * lax.rsqrt(var + LN_EPS) * g + b


def _dot(a, b):
    return jnp.dot(a, b, preferred_element_type=F32)


def _one_plus_tanh_gelu_arg(x):
    return 1.0 + jnp.tanh(x * (GELU_C0 + GELU_C1 * (x * x)))


def _sigmoid(x):
    return 0.5 * (1.0 + jnp.tanh(0.5 * x))


def _time_of_row(rho):
    return VROWS * (rho & (SUBLANES - 1)) + (rho >> 3)


def _perm_index():
    rho = np.arange(GROUP)
    return VROWS * (rho % SUBLANES) + rho // SUBLANES


def _sublane_shift(cur, prev):
    row = lax.broadcasted_iota(jnp.int32, cur.shape, 0)
    return jnp.where(row == 0, pltpu.roll(prev, 1, axis=0), pltpu.roll(cur, 1, axis=0))


def _const_spec(shape, lead=None):
    nd = len(shape)
    if lead is None:
        return pl.BlockSpec(shape, lambda i: (0,) * nd, pipeline_mode=pl.Buffered(1))
    return pl.BlockSpec((None,) + tuple(shape), lambda i: (lead,) + (0,) * nd,
                        pipeline_mode=pl.Buffered(1))


def _weight_cast_stream(ffn_w_up, ffn_w_down, layer, n_steps):
    _, d, up_cols = ffn_w_up.shape
    _, d_ff, _ = ffn_w_down.shape
    assert d % (n_steps * BF16_ROWS) == 0
    up_rows = d // n_steps
    dn_blocks = max(nb for nb in range(1, n_steps + 1)
                    if d_ff % nb == 0 and (d_ff // nb) % BF16_ROWS == 0)
    dn_rows = d_ff // dn_blocks
    dn_index = lambda i: jnp.minimum(i, dn_blocks - 1)
    in_specs = [pl.BlockSpec((None, up_rows, up_cols), lambda i: (layer, i, 0)),
                pl.BlockSpec((None, dn_rows, d), lambda i: (layer, dn_index(i), 0))]
    out_specs = [pl.BlockSpec((up_rows, up_cols), lambda i: (i, 0)),
                 pl.BlockSpec((dn_rows, d), lambda i: (dn_index(i), 0))]
    out_shapes = [jax.ShapeDtypeStruct((d, up_cols), BF16), jax.ShapeDtypeStruct((d_ff, d), BF16)]
    return in_specs, out_specs, out_shapes


def _cast_weight_slabs(wup_in_ref, wdn_in_ref, wup_out_ref, wdn_out_ref):
    wup_out_ref[...] = wup_in_ref[...].astype(BF16)
    wdn_out_ref[...] = wdn_in_ref[...].astype(BF16)


def _ffn_rows(rows, x_ref, wup_ref, cw_ref, cb_ref, wdn_ref, g, b, o_ref, act_ref, carry_ref,
              stage_ref, *, alpha, layer, d_ff, col_tile):
    tm = rows.stop - rows.start
    n_groups = tm // GROUP
    x = x_ref[rows, :]
    xb = x.astype(BF16)

    def conv_chunk(col, scale):
        h = _dot(xb, wup_ref[:, col:col + col_tile])
        w = cw_ref[:, col:col + col_tile]
        bias = cb_ref[layer:layer + 1, col:col + col_tile]
        if scale != 1.0:
            w, bias = w * scale, bias * scale
        prev = [carry_ref[0:SUBLANES, col:col + col_tile],
                carry_ref[SUBLANES:2 * SUBLANES, col:col + col_tile]]
        h1, h2 = [], []
        for gi in range(n_groups):
            hg = h[gi * GROUP:(gi + 1) * GROUP]
            cur = [hg[GROUP - 2 * SUBLANES:GROUP - SUBLANES], hg[GROUP - SUBLANES:]]
            wrapped = [_sublane_shift(cur[i], prev[i]) for i in range(2)]
            h1 += [wrapped[1], hg[:GROUP - SUBLANES]]
            h2 += [wrapped[0], wrapped[1], hg[:GROUP - 2 * SUBLANES]]
            prev = cur
        carry_ref[0:SUBLANES, col:col + col_tile] = prev[0]
        carry_ref[SUBLANES:2 * SUBLANES, col:col + col_tile] = prev[1]
        y = w[2:3] * h + bias
        y = y + w[1:2] * jnp.concatenate(h1, axis=0)
        y = y + w[0:1] * jnp.concatenate(h2, axis=0)
        return y

    for c in range(d_ff // col_tile):
        gate = conv_chunk(c * col_tile, 1.0)
        half_val = conv_chunk(d_ff + c * col_tile, 0.5)
        act = gate * _one_plus_tanh_gelu_arg(gate) * half_val
        act_ref[rows, c * col_tile:(c + 1) * col_tile] = act.astype(BF16)

    ffn = _dot(act_ref[rows, :], wdn_ref[...])
    y = _layer_norm(alpha * x + ffn, g, b)
    if stage_ref is None:
        o_ref[rows, :] = y
    else:
        d = y.shape[1]
        for c in range(d // LANES):
            stage_ref[c] = y[:, c * LANES:(c + 1) * LANES]
        for c in range(d // LANES):
            for gi in range(n_groups):
                for s in range(SUBLANES):
                    for half in range(VROWS // SUBLANES):
                        src = gi * GROUP + half * SUBLANES * SUBLANES + s
                        dst = rows.start + gi * GROUP + VROWS * s + SUBLANES * half
                        o_ref[dst:dst + SUBLANES, c * LANES:(c + 1) * LANES] = (
                            stage_ref[c, pl.ds(src, SUBLANES, stride=SUBLANES), :])


def _ffn_kernel(x_ref, wup_ref, cw_ref, cb_ref, wdn_ref, g_ref, b_ref, o_ref,
                act_ref, carry_ref, *rest, tiles_per_batch, sub_tile, **static):
    tm = x_ref.shape[0]

    @pl.when(pl.program_id(0) % tiles_per_batch == 0)
    def _():
        carry_ref[...] = jnp.zeros_like(carry_ref)

    layer = static["layer"]
    g = g_ref[layer, 1:2, :]
    b = b_ref[layer, 1:2, :]
    stage_ref = rest[0] if rest else None
    for t in range(tm // sub_tile):
        _ffn_rows(slice(t * sub_tile, (t + 1) * sub_tile), x_ref, wup_ref, cw_ref, cb_ref, wdn_ref,
                  g, b, o_ref, act_ref, carry_ref, stage_ref, **static)


def _ffn_call(x2d, w_up_bf, conv_w, conv_b, w_down_bf, ln_g, ln_b, *, layer, alpha, seq,
              natural_out):
    n, d = x2d.shape
    d_ff = w_down_bf.shape[0]
    tm = FFN_ROW_TILE
    kern = functools.partial(_ffn_kernel, alpha=alpha, layer=layer, tiles_per_batch=seq // tm,
                             sub_tile=FFN_SUB_TILE, d_ff=d_ff, col_tile=FFN_COL_TILE)
    scratch = [pltpu.VMEM((tm, d_ff), BF16), pltpu.VMEM((2 * SUBLANES, 2 * d_ff), F32)]
    if natural_out:
        scratch.append(pltpu.VMEM((d // LANES, FFN_SUB_TILE, LANES), F32))
    return pl.pallas_call(
        kern,
        grid=(n // tm,),
        in_specs=[
            pl.BlockSpec((tm, d), lambda i: (i, 0)),
            _const_spec((d, 2 * d_ff)),
            _const_spec((3, 2 * d_ff), lead=layer),
            _const_spec(conv_b.shape),
            _const_spec((d_ff, d)),
            _const_spec(ln_g.shape),
            _const_spec(ln_b.shape),
        ],
        out_specs=pl.BlockSpec((tm, d), lambda i: (i, 0)),
        out_shape=jax.ShapeDtypeStruct((n, d), F32),
        scratch_shapes=scratch,
        compiler_params=pltpu.CompilerParams(
            dimension_semantics=("arbitrary",), vmem_limit_bytes=VMEM_LIMIT_BYTES),
        name="conv_ffn_out" if natural_out else "conv_ffn",
    )(x2d, w_up_bf, conv_w, conv_b, w_down_bf, ln_g, ln_b)


def _mixer_kernel(*refs, alpha, tiles_per_batch, width, n_col_blocks):
    x_refs = refs[:n_col_blocks]
    (win_f32_ref, acw_ref, acb_ref, ang_ref, anb_ref, bng_ref, bnb_ref,
     ws_ref, sb_ref, wout_f32_ref, g_ref, b_ref, ffn_up_ref, ffn_dn_ref,
     o_ref, ffn_up_bf_ref, ffn_dn_bf_ref, carry_ref, win_ref, wout_ref) = refs[n_col_blocks:]
    tm = o_ref.shape[0]
    n_groups = tm // GROUP
    _cast_weight_slabs(ffn_up_ref, ffn_dn_ref, ffn_up_bf_ref, ffn_dn_bf_ref)

    @pl.when(pl.program_id(0) == 0)
    def _():
        win_ref[...] = win_f32_ref[...].astype(BF16)
        wout_ref[...] = wout_f32_ref[...].astype(BF16)

    @pl.when(pl.program_id(0) % tiles_per_batch == 0)
    def _():
        carry_ref[...] = jnp.zeros_like(carry_ref)

    x = jnp.concatenate(
        [jnp.concatenate(
            [x_refs[c][pl.ds(g * GROUP + j, SUBLANES, stride=VROWS), :]
             for g in range(n_groups) for j in range(VROWS)], axis=0)
         for c in range(n_col_blocks)], axis=1)
    xb = x.astype(BF16)
    a_val = _dot(xb, win_ref[:, 0 * width:1 * width])
    a_gate = _dot(xb, win_ref[:, 1 * width:2 * width])
    b_u = _dot(xb, win_ref[:, 2 * width:3 * width])
    b_v = _dot(xb, win_ref[:, 3 * width:4 * width])

    a = a_val * _sigmoid(a_gate)
    ext = jnp.concatenate([carry_ref[...], a], axis=0)
    carry_ref[...] = a[tm - GROUP:]
    n_taps = acw_ref.shape[0]
    max_sub = (n_taps - 1) // VROWS + 1

    def sublane_shifted(z):
        out = []
        for g in range(n_groups + 1):
            zg = z[g * GROUP:(g + 1) * GROUP]
            zp = z[(g - 1) * GROUP:g * GROUP] if g > 0 else zg
            out += [_sublane_shift(zg[j * SUBLANES:(j + 1) * SUBLANES],
                                   zp[j * SUBLANES:(j + 1) * SUBLANES]) for j in range(VROWS)]
        return jnp.concatenate(out, axis=0)

    shifted = [ext]
    for _ in range(max_sub):
        shifted.append(sublane_shifted(shifted[-1]))
    acw = acw_ref[...]
    conv = jnp.broadcast_to(acb_ref[...], (tm, width))
    for k in range(n_taps):
        q, r = divmod(k, VROWS)
        lo = shifted[q][GROUP:]
        if r == 0:
            term = lo
        else:
            hi = shifted[q + 1][GROUP:]
            cut = (VROWS - r) * SUBLANES
            parts = []
            for g in range(n_groups):
                parts += [hi[g * GROUP + cut:(g + 1) * GROUP], lo[g * GROUP:g * GROUP + cut]]
            term = jnp.concatenate(parts, axis=0)
        j = n_taps - 1 - k
        conv = conv + acw[j:j + 1] * term
    a_out = _layer_norm(conv, ang_ref[...], anb_ref[...])
    a_out = a_out * _sigmoid(a_out)

    gelu_v = 0.5 * b_v * _one_plus_tanh_gelu_arg(b_v)
    v = _layer_norm(gelu_v, bng_ref[...], bnb_ref[...]).astype(BF16)
    ti = _time_of_row(lax.broadcasted_iota(jnp.int32, (GROUP, GROUP), 0))
    tj = _time_of_row(lax.broadcasted_iota(jnp.int32, (GROUP, GROUP), 1))
    gd = width // GMLP_GROUPS
    ws = [(jnp.where(ti >= tj, ws_ref[g], 0.0) * 0.5).astype(BF16) for g in range(GMLP_GROUPS)]
    half_bias = sb_ref[...] * 0.5
    chunks = []
    for c in range(n_groups):
        rows = slice(c * GROUP, (c + 1) * GROUP)
        cols = [_dot(ws[g], v[rows, g * gd:(g + 1) * gd]) for g in range(GMLP_GROUPS)]
        chunks.append(jnp.concatenate(cols, axis=1) + half_bias)
    b_out = b_u * _one_plus_tanh_gelu_arg(b_u) * jnp.concatenate(chunks, axis=0)

    mix = _dot(jnp.concatenate([a_out.astype(BF16), b_out.astype(BF16)], axis=1), wout_ref[...])
    o_ref[...] = _layer_norm(alpha * x + mix, g_ref[0, 0:1, :], b_ref[0, 0:1, :])


def _mixer_call(x2d, w_in, a_conv_w, a_conv_b, a_norm_g, a_norm_b, b_norm_g, b_norm_b,
                b_spatial_w, b_spatial_b, w_out, ln_g, ln_b, ffn_w_up, ffn_w_down, *, alpha, seq):
    n, d = x2d.shape
    width = a_conv_w.shape[2]
    gd = width // GMLP_GROUPS
    tm = ROW_TILE
    n_col_blocks = d // LANES
    perm = _perm_index()
    ws_perm = b_spatial_w[0][:, perm][:, :, perm]
    sbias = jnp.repeat(b_spatial_b[0].T[perm], gd, axis=1)
    kern = functools.partial(_mixer_kernel, alpha=alpha, tiles_per_batch=seq // tm, width=width,
                             n_col_blocks=n_col_blocks)
    x_specs = [pl.BlockSpec((tm, LANES), functools.partial(lambda c, i: (i, c), c))
               for c in range(n_col_blocks)]
    cast_in, cast_out, cast_shapes = _weight_cast_stream(ffn_w_up, ffn_w_down, 0, n // tm)
    return pl.pallas_call(
        kern,
        grid=(n // tm,),
        in_specs=x_specs + [
            _const_spec(w_in.shape[1:], lead=0),
            _const_spec(a_conv_w.shape[1:], lead=0),
            _const_spec(a_conv_b.shape),
            _const_spec(a_norm_g.shape),
            _const_spec(a_norm_b.shape),
            _const_spec(b_norm_g.shape),
            _const_spec(b_norm_b.shape),
            _const_spec(ws_perm.shape),
            _const_spec(sbias.shape),
            _const_spec(w_out.shape[1:], lead=0),
            _const_spec(ln_g.shape),
            _const_spec(ln_b.shape),
        ] + cast_in,
        out_specs=[pl.BlockSpec((tm, d), lambda i: (i, 0))] + cast_out,
        out_shape=[jax.ShapeDtypeStruct((n, d), F32)] + cast_shapes,
        scratch_shapes=[pltpu.VMEM((GROUP, width), F32),
                        pltpu.VMEM(w_in.shape[1:], BF16),
                        pltpu.VMEM(w_out.shape[1:], BF16)],
        compiler_params=pltpu.CompilerParams(
            dimension_semantics=("arbitrary",), vmem_limit_bytes=VMEM_LIMIT_BYTES),
        name="conv_gmlp_mixer",
    )(*([x2d] * n_col_blocks), w_in, a_conv_w, a_conv_b, a_norm_g, a_norm_b, b_norm_g, b_norm_b,
      ws_perm, sbias, w_out, ln_g, ln_b, ffn_w_up, ffn_w_down)


def _attn_kernel(sinks_ref, x_ref, wqkv_f32_ref, bqkv_ref, wo_f32_ref, g_ref, b_ref,
                 ffn_up_ref, ffn_dn_ref, o_ref, ffn_up_bf_ref, ffn_dn_bf_ref,
                 attn_ref, kv_carry_ref, wqkv_ref, wo_ref, *, alpha, tiles_per_batch):
    _cast_weight_slabs(ffn_up_ref, ffn_dn_ref, ffn_up_bf_ref, ffn_dn_bf_ref)
    tm = x_ref.shape[0]
    d_q = N_KV_HEADS * Q_PER_KV * HEAD_DIM
    d_kv = N_KV_HEADS * HEAD_DIM
    pair_w = 2 * HEAD_DIM
    pairs = Q_PER_KV // 2
    first = pl.program_id(0) % tiles_per_batch == 0

    @pl.when(first)
    def _():
        kv_carry_ref[...] = jnp.zeros_like(kv_carry_ref)

    @pl.when(pl.program_id(0) == 0)
    def _():
        wqkv_ref[...] = wqkv_f32_ref[...].astype(BF16)
        wo_ref[...] = wo_f32_ref[...].astype(BF16)

    x = x_ref[...]
    qkv = _dot(x.astype(BF16), wqkv_ref[...]) + bqkv_ref[...]
    q = (qkv[:, :d_q] * (1.0 / math.sqrt(HEAD_DIM))).astype(BF16)
    kv = qkv[:, d_q:]
    kv_all = jnp.concatenate([kv_carry_ref[...], kv], axis=0)
    kv_carry_ref[...] = kv[tm - GROUP:]
    k_all = kv_all[:, :d_kv]
    v_all = kv_all[:, d_kv:]
    k_sw = pltpu.roll(k_all, HEAD_DIM, axis=1)
    v_sw = pltpu.roll(v_all, HEAD_DIM, axis=1)
    low = lax.broadcasted_iota(jnp.int32, k_all.shape, 1) < HEAD_DIM
    low_q = lax.broadcasted_iota(jnp.int32, (GROUP, pair_w), 1) < HEAD_DIM

    tq = _time_of_row(lax.broadcasted_iota(jnp.int32, (GROUP, GROUP), 0))
    tk = _time_of_row(lax.broadcasted_iota(jnp.int32, (GROUP, GROUP), 1))
    use_cur = tk <= tq
    prev_fill = jnp.where(first, -jnp.inf, 0.0).astype(F32)
    nt = (((1,), (1,)), ((), ()))

    for kvh in range(N_KV_HEADS):
        k_src, k_alt = (k_all, k_sw) if kvh == 0 else (k_sw, k_all)
        v_src, v_alt = (v_all, v_sw) if kvh == 0 else (v_sw, v_all)
        k_even = jnp.where(low, k_src, 0.0).astype(BF16)
        k_odd = jnp.where(low, 0.0, k_alt).astype(BF16)
        v_even = jnp.where(low, v_src, 0.0).astype(BF16)
        v_odd = jnp.where(low, 0.0, v_alt).astype(BF16)
        for n in range(tm // GROUP):
            kr = slice(n * GROUP, (n + 2) * GROUP)
            qr = slice(n * GROUP, (n + 1) * GROUP)
            keys = jnp.concatenate([k_even[kr], k_odd[kr]], axis=0)
            vals = jnp.concatenate([v_even[kr], v_odd[kr]], axis=0)
            col0 = kvh * Q_PER_KV * HEAD_DIM
            qs = jnp.concatenate(
                [q[qr, col0 + p * pair_w:col0 + (p + 1) * pair_w] for p in range(pairs)], axis=0)
            s = lax.dot_general(qs, keys, nt, preferred_element_type=F32)
            prob_rows, inv_den = [], []
            for p in range(pairs):
                halves, inv = [], []
                for e in range(2):
                    c0 = e * 2 * GROUP
                    s_prev = s[p * GROUP:(p + 1) * GROUP, c0:c0 + GROUP]
                    s_cur = s[p * GROUP:(p + 1) * GROUP, c0 + GROUP:c0 + 2 * GROUP]
                    if n == 0:
                        s_prev = s_prev + prev_fill
                    merged = jnp.where(use_cur, s_cur, s_prev)
                    sink = sinks_ref[0, kvh * Q_PER_KV + 2 * p + e]
                    m = jnp.maximum(jnp.max(merged, axis=1, keepdims=True), sink)
                    pe = jnp.exp(merged - m)
                    den = jnp.sum(pe, axis=1, keepdims=True) + jnp.exp(sink - m)
                    inv.append(1.0 / den)
                    halves += [jnp.where(use_cur, 0.0, pe).astype(BF16),
                               jnp.where(use_cur, pe, 0.0).astype(BF16)]
                prob_rows.append(jnp.concatenate(halves, axis=1))
                inv_den.append(jnp.where(low_q, inv[0], inv[1]))
            probs = jnp.concatenate(prob_rows, axis=0)
            o = _dot(probs, vals)
            for p in range(pairs):
                attn_ref[qr, col0 + p * pair_w:col0 + (p + 1) * pair_w] = (
                    o[p * GROUP:(p + 1) * GROUP] * inv_den[p]).astype(BF16)

    mix = _dot(attn_ref[...], wo_ref[...])
    o_ref[...] = _layer_norm(alpha * x + mix, g_ref[1, 0:1, :], b_ref[1, 0:1, :])


def _attn_call(x2d, w_qkv, b_qkv, sinks, w_o, ln_g, ln_b, ffn_w_up, ffn_w_down, *, alpha, seq):
    n, d = x2d.shape
    d_q = w_o.shape[1]
    tm = ROW_TILE
    kern = functools.partial(_attn_kernel, alpha=alpha, tiles_per_batch=seq // tm)
    cast_in, cast_out, cast_shapes = _weight_cast_stream(ffn_w_up, ffn_w_down, 1, n // tm)
    return pl.pallas_call(
        kern,
        grid=(n // tm,),
        in_specs=[
            pl.BlockSpec(memory_space=pltpu.SMEM),
            pl.BlockSpec((tm, d), lambda i: (i, 0)),
            _const_spec(w_qkv.shape[1:], lead=0),
            _const_spec(b_qkv.shape),
            _const_spec(w_o.shape[1:], lead=0),
            _const_spec(ln_g.shape),
            _const_spec(ln_b.shape),
        ] + cast_in,
        out_specs=[pl.BlockSpec((tm, d), lambda i: (i, 0))] + cast_out,
        out_shape=[jax.ShapeDtypeStruct((n, d), F32)] + cast_shapes,
        scratch_shapes=[
            pltpu.VMEM((tm, d_q), BF16),
            pltpu.VMEM((GROUP, 2 * N_KV_HEADS * HEAD_DIM), F32),
            pltpu.VMEM(w_qkv.shape[1:], BF16),
            pltpu.VMEM(w_o.shape[1:], BF16),
        ],
        compiler_params=pltpu.CompilerParams(
            dimension_semantics=("arbitrary",), vmem_limit_bytes=VMEM_LIMIT_BYTES),
        name="swa_sink_attention",
    )(sinks, x2d, w_qkv, b_qkv, w_o, ln_g, ln_b, ffn_w_up, ffn_w_down)


def kernel(x, ab_w_in, a_conv_w, a_conv_b, a_norm_g, a_norm_b, b_norm_g, b_norm_b, b_spatial_w, b_spatial_b, ab_w_out, c_w_qkv, c_b_qkv, c_sinks, c_w_o, ffn_w_up, ffn_conv_w, ffn_conv_b, ffn_w_down, ln_g, ln_b):
    bsz, seq, d = x.shape
    depth = ffn_w_up.shape[0]
    assert depth == 2 and seq % FFN_ROW_TILE == 0 and seq % ROW_TILE == 0
    alpha = float(np.float32((2.0 * depth) ** 0.25))
    h = x.reshape(bsz * seq, d)
    h, w_up_bf, w_down_bf = _mixer_call(
        h, ab_w_in, a_conv_w, a_conv_b, a_norm_g, a_norm_b, b_norm_g, b_norm_b, b_spatial_w,
        b_spatial_b, ab_w_out, ln_g, ln_b, ffn_w_up, ffn_w_down, alpha=alpha, seq=seq)
    h = _ffn_call(h, w_up_bf, ffn_conv_w, ffn_conv_b, w_down_bf, ln_g, ln_b,
                  layer=0, alpha=alpha, seq=seq, natural_out=False)
    h, w_up_bf, w_down_bf = _attn_call(h, c_w_qkv, c_b_qkv, c_sinks, c_w_o, ln_g, ln_b,
                                       ffn_w_up, ffn_w_down, alpha=alpha, seq=seq)
    h = _ffn_call(h, w_up_bf, ffn_conv_w, ffn_conv_b, w_down_bf, ln_g, ln_b,
                  layer=1, alpha=alpha, seq=seq, natural_out=True)
    return h.reshape(bsz, seq, d)
```

```python
import functools
import math

import jax
import jax.numpy as jnp
import numpy as np
from jax import lax
from jax.experimental import pallas as pl
from jax.experimental.pallas import tpu as pltpu

F32 = jnp.float32
BF16 = jnp.bfloat16

LN_EPS = 1e-5
SUBLANES = 8
LANES = 128
BF16_ROWS = 16
GROUP = 128
VROWS = GROUP // SUBLANES
GMLP_GROUPS = 4
HEAD_DIM = 64
Q_PER_KV = 8
N_KV_HEADS = 2
VMEM_LIMIT_BYTES = 56 * 1024 * 1024

ROW_TILE = 512
ATTN_ROW_TILE = 1024
ATTN_SUB_TILE = 512
FFN_ROW_TILE = 1024
FFN_SUB_TILE = 512
FFN_COL_TILE = 256

GELU_C0 = math.sqrt(2.0 / math.pi)
GELU_C1 = 0.044715 * GELU_C0


def _layer_norm(z, g, b):
    mu = jnp.mean(z, axis=-1, keepdims=True)
    d = z - mu
    var = jnp.mean(d * d, axis=-1, keepdims=True)
    return d * lax.rsqrt(var + LN_EPS) * g + b


def _dot(a, b):
    return jnp.dot(a, b, preferred_element_type=F32)


def _one_plus_tanh_gelu_arg(x):
    return 1.0 + jnp.tanh(x * (GELU_C0 + GELU_C1 * (x * x)))


def _sigmoid(x):
    return 0.5 * (1.0 + jnp.tanh(0.5 * x))


def _time_of_row(rho):
    return VROWS * (rho & (SUBLANES - 1)) + (rho >> 3)


def _perm_index():
    rho = np.arange(GROUP)
    return VROWS * (rho % SUBLANES) + rho // SUBLANES


def _sublane_shift(cur, prev):
    row = lax.broadcasted_iota(jnp.int32, cur.shape, 0)
    return jnp.where(row == 0, pltpu.roll(prev, 1, axis=0), pltpu.roll(cur, 1, axis=0))


def _const_spec(shape, lead=None):
    nd = len(shape)
    if lead is None:
        return pl.BlockSpec(shape, lambda i: (0,) * nd, pipeline_mode=pl.Buffered(1))
    return pl.BlockSpec((None,) + tuple(shape), lambda i: (lead,) + (0,) * nd,
                        pipeline_mode=pl.Buffered(1))


def _weight_cast_stream(ffn_w_up, ffn_w_down, layer, n_steps):
    _, d, up_cols = ffn_w_up.shape
    _, d_ff, _ = ffn_w_down.shape
    assert d % (n_steps * BF16_ROWS) == 0
    up_rows = d // n_steps
    dn_blocks = max(nb for nb in range(1, n_steps + 1)
                    if d_ff % nb == 0 and (d_ff // nb) % BF16_ROWS == 0)
    dn_rows = d_ff // dn_blocks
    dn_index = lambda i: jnp.minimum(i, dn_blocks - 1)
    in_specs = [pl.BlockSpec((None, up_rows, up_cols), lambda i: (layer, i, 0)),
                pl.BlockSpec((None, dn_rows, d), lambda i: (layer, dn_index(i), 0))]
    out_specs = [pl.BlockSpec((up_rows, up_cols), lambda i: (i, 0)),
                 pl.BlockSpec((dn_rows, d), lambda i: (dn_index(i), 0))]
    out_shapes = [jax.ShapeDtypeStruct((d, up_cols), BF16), jax.ShapeDtypeStruct((d_ff, d), BF16)]
    return in_specs, out_specs, out_shapes


def _cast_weight_slabs(wup_in_ref, wdn_in_ref, wup_out_ref, wdn_out_ref):
    wup_out_ref[...] = wup_in_ref[...].astype(BF16)
    wdn_out_ref[...] = wdn_in_ref[...].astype(BF16)


def _ffn_rows(rows, x_ref, wup_ref, cw_ref, cb_ref, wdn_ref, g, b, o_ref, act_ref, carry_ref,
              stage_ref, *, alpha, layer, d_ff, col_tile):
    tm = rows.stop - rows.start
    n_groups = tm // GROUP
    x = x_ref[rows, :]
    xb = x.astype(BF16)

    def conv_chunk(col, scale):
        h = _dot(xb, wup_ref[:, col:col + col_tile])
        w = cw_ref[:, col:col + col_tile]
        bias = cb_ref[layer:layer + 1, col:col + col_tile]
        if scale != 1.0:
            w, bias = w * scale, bias * scale
        prev = [carry_ref[0:SUBLANES, col:col + col_tile],
                carry_ref[SUBLANES:2 * SUBLANES, col:col + col_tile]]
        h1, h2 = [], []
        for gi in range(n_groups):
            hg = h[gi * GROUP:(gi + 1) * GROUP]
            cur = [hg[GROUP - 2 * SUBLANES:GROUP - SUBLANES], hg[GROUP - SUBLANES:]]
            wrapped = [_sublane_shift(cur[i], prev[i]) for i in range(2)]
            h1 += [wrapped[1], hg[:GROUP - SUBLANES]]
            h2 += [wrapped[0], wrapped[1], hg[:GROUP - 2 * SUBLANES]]
            prev = cur
        carry_ref[0:SUBLANES, col:col + col_tile] = prev[0]
        carry_ref[SUBLANES:2 * SUBLANES, col:col + col_tile] = prev[1]
        y = w[2:3] * h + bias
        y = y + w[1:2] * jnp.concatenate(h1, axis=0)
        y = y + w[0:1] * jnp.concatenate(h2, axis=0)
        return y

    for c in range(d_ff // col_tile):
        gate = conv_chunk(c * col_tile, 1.0)
        half_val = conv_chunk(d_ff + c * col_tile, 0.5)
        act = gate * _one_plus_tanh_gelu_arg(gate) * half_val
        act_ref[rows, c * col_tile:(c + 1) * col_tile] = act.astype(BF16)

    ffn = _dot(act_ref[rows, :], wdn_ref[...])
    y = _layer_norm(alpha * x + ffn, g, b)
    if stage_ref is None:
        o_ref[rows, :] = y
    else:
        d = y.shape[1]
        for c in range(d // LANES):
            stage_ref[c] = y[:, c * LANES:(c + 1) * LANES]
        for c in range(d // LANES):
            for gi in range(n_groups):
                for s in range(SUBLANES):
                    for half in range(VROWS // SUBLANES):
                        src = gi * GROUP + half * SUBLANES * SUBLANES + s
                        dst = rows.start + gi * GROUP + VROWS * s + SUBLANES * half
                        o_ref[dst:dst + SUBLANES, c * LANES:(c + 1) * LANES] = (
                            stage_ref[c, pl.ds(src, SUBLANES, stride=SUBLANES), :])


def _ffn_kernel(x_ref, wup_ref, cw_ref, cb_ref, wdn_ref, g_ref, b_ref, o_ref,
                act_ref, carry_ref, *rest, tiles_per_batch, sub_tile, **static):
    tm = x_ref.shape[0]

    @pl.when(pl.program_id(0) % tiles_per_batch == 0)
    def _():
        carry_ref[...] = jnp.zeros_like(carry_ref)

    layer = static["layer"]
    g = g_ref[layer, 1:2, :]
    b = b_ref[layer, 1:2, :]
    stage_ref = rest[0] if rest else None
    for t in range(tm // sub_tile):
        _ffn_rows(slice(t * sub_tile, (t + 1) * sub_tile), x_ref, wup_ref, cw_ref, cb_ref, wdn_ref,
                  g, b, o_ref, act_ref, carry_ref, stage_ref, **static)


def _ffn_call(x2d, w_up_bf, conv_w, conv_b, w_down_bf, ln_g, ln_b, *, layer, alpha, seq,
              natural_out):
    n, d = x2d.shape
    d_ff = w_down_bf.shape[0]
    tm = FFN_ROW_TILE
    kern = functools.partial(_ffn_kernel, alpha=alpha, layer=layer, tiles_per_batch=seq // tm,
                             sub_tile=FFN_SUB_TILE, d_ff=d_ff, col_tile=FFN_COL_TILE)
    scratch = [pltpu.VMEM((tm, d_ff), BF16), pltpu.VMEM((2 * SUBLANES, 2 * d_ff), F32)]
    if natural_out:
        scratch.append(pltpu.VMEM((d // LANES, FFN_SUB_TILE, LANES), F32))
    return pl.pallas_call(
        kern,
        grid=(n // tm,),
        in_specs=[
            pl.BlockSpec((tm, d), lambda i: (i, 0)),
            _const_spec((d, 2 * d_ff)),
            _const_spec((3, 2 * d_ff), lead=layer),
            _const_spec(conv_b.shape),
            _const_spec((d_ff, d)),
            _const_spec(ln_g.shape),
            _const_spec(ln_b.shape),
        ],
        out_specs=pl.BlockSpec((tm, d), lambda i: (i, 0)),
        out_shape=jax.ShapeDtypeStruct((n, d), F32),
        scratch_shapes=scratch,
        compiler_params=pltpu.CompilerParams(
            dimension_semantics=("arbitrary",), vmem_limit_bytes=VMEM_LIMIT_BYTES),
        name="conv_ffn_out" if natural_out else "conv_ffn",
    )(x2d, w_up_bf, conv_w, conv_b, w_down_bf, ln_g, ln_b)


def _mixer_kernel(*refs, alpha, tiles_per_batch, width, n_col_blocks):
    x_refs = refs[:n_col_blocks]
    (win_f32_ref, acw_ref, acb_ref, ang_ref, anb_ref, bng_ref, bnb_ref,
     ws_ref, sb_ref, wout_f32_ref, g_ref, b_ref, ffn_up_ref, ffn_dn_ref,
     o_ref, ffn_up_bf_ref, ffn_dn_bf_ref, carry_ref, win_ref, wout_ref) = refs[n_col_blocks:]
    tm = o_ref.shape[0]
    n_groups = tm // GROUP
    _cast_weight_slabs(ffn_up_ref, ffn_dn_ref, ffn_up_bf_ref, ffn_dn_bf_ref)

    @pl.when(pl.program_id(0) == 0)
    def _():
        win_ref[...] = win_f32_ref[...].astype(BF16)
        wout_ref[...] = wout_f32_ref[...].astype(BF16)

    @pl.when(pl.program_id(0) % tiles_per_batch == 0)
    def _():
        carry_ref[...] = jnp.zeros_like(carry_ref)

    x = jnp.concatenate(
        [jnp.concatenate(
            [x_refs[c][pl.ds(g * GROUP + j, SUBLANES, stride=VROWS), :]
             for g in range(n_groups) for j in range(VROWS)], axis=0)
         for c in range(n_col_blocks)], axis=1)
    xb = x.astype(BF16)
    a_val = _dot(xb, win_ref[:, 0 * width:1 * width])
    a_gate = _dot(xb, win_ref[:, 1 * width:2 * width])
    b_u = _dot(xb, win_ref[:, 2 * width:3 * width])
    b_v = _dot(xb, win_ref[:, 3 * width:4 * width])

    a = a_val * _sigmoid(a_gate)
    ext = jnp.concatenate([carry_ref[...], a], axis=0)
    carry_ref[...] = a[tm - GROUP:]
    n_taps = acw_ref.shape[0]
    max_sub = (n_taps - 1) // VROWS + 1

    def sublane_shifted(z):
        out = []
        for g in range(n_groups + 1):
            zg = z[g * GROUP:(g + 1) * GROUP]
            zp = z[(g - 1) * GROUP:g * GROUP] if g > 0 else zg
            out += [_sublane_shift(zg[j * SUBLANES:(j + 1) * SUBLANES],
                                   zp[j * SUBLANES:(j + 1) * SUBLANES]) for j in range(VROWS)]
        return jnp.concatenate(out, axis=0)

    shifted = [ext]
    for _ in range(max_sub):
        shifted.append(sublane_shifted(shifted[-1]))
    acw = acw_ref[...]
    conv = jnp.broadcast_to(acb_ref[...], (tm, width))
    for k in range(n_taps):
        q, r = divmod(k, VROWS)
        lo = shifted[q][GROUP:]
        if r == 0:
            term = lo
        else:
            hi = shifted[q + 1][GROUP:]
            cut = (VROWS - r) * SUBLANES
            parts = []
            for g in range(n_groups):
                parts += [hi[g * GROUP + cut:(g + 1) * GROUP], lo[g * GROUP:g * GROUP + cut]]
            term = jnp.concatenate(parts, axis=0)
        j = n_taps - 1 - k
        conv = conv + acw[j:j + 1] * term
    a_out = _layer_norm(conv, ang_ref[...], anb_ref[...])
    a_out = a_out * _sigmoid(a_out)

    gelu_v = 0.5 * b_v * _one_plus_tanh_gelu_arg(b_v)
    v = _layer_norm(gelu_v, bng_ref[...], bnb_ref[...]).astype(BF16)
    ti = _time_of_row(lax.broadcasted_iota(jnp.int32, (GROUP, GROUP), 0))
    tj = _time_of_row(lax.broadcasted_iota(jnp.int32, (GROUP, GROUP), 1))
    gd = width // GMLP_GROUPS
    ws = [(jnp.where(ti >= tj, ws_ref[g], 0.0) * 0.5).astype(BF16) for g in range(GMLP_GROUPS)]
    half_bias = sb_ref[...] * 0.5
    chunks = []
    for c in range(n_groups):
        rows = slice(c * GROUP, (c + 1) * GROUP)
        cols = [_dot(ws[g], v[rows, g * gd:(g + 1) * gd]) for g in range(GMLP_GROUPS)]
        chunks.append(jnp.concatenate(cols, axis=1) + half_bias)
    b_out = b_u * _one_plus_tanh_gelu_arg(b_u) * jnp.concatenate(chunks, axis=0)

    mix = _dot(jnp.concatenate([a_out.astype(BF16), b_out.astype(BF16)], axis=1), wout_ref[...])
    o_ref[...] = _layer_norm(alpha * x + mix, g_ref[0, 0:1, :], b_ref[0, 0:1, :])


def _mixer_call(x2d, w_in, a_conv_w, a_conv_b, a_norm_g, a_norm_b, b_norm_g, b_norm_b,
                b_spatial_w, b_spatial_b, w_out, ln_g, ln_b, ffn_w_up, ffn_w_down, *, alpha, seq):
    n, d = x2d.shape
    width = a_conv_w.shape[2]
    gd = width // GMLP_GROUPS
    tm = ROW_TILE
    n_col_blocks = d // LANES
    perm = _perm_index()
    ws_perm = b_spatial_w[0][:, perm][:, :, perm]
    sbias = jnp.repeat(b_spatial_b[0].T[perm], gd, axis=1)
    kern = functools.partial(_mixer_kernel, alpha=alpha, tiles_per_batch=seq // tm, width=width,
                             n_col_blocks=n_col_blocks)
    x_specs = [pl.BlockSpec((tm, LANES), functools.partial(lambda c, i: (i, c), c))
               for c in range(n_col_blocks)]
    cast_in, cast_out, cast_shapes = _weight_cast_stream(ffn_w_up, ffn_w_down, 0, n // tm)
    return pl.pallas_call(
        kern,
        grid=(n // tm,),
        in_specs=x_specs + [
            _const_spec(w_in.shape[1:], lead=0),
            _const_spec(a_conv_w.shape[1:], lead=0),
            _const_spec(a_conv_b.shape),
            _const_spec(a_norm_g.shape),
            _const_spec(a_norm_b.shape),
            _const_spec(b_norm_g.shape),
            _const_spec(b_norm_b.shape),
            _const_spec(ws_perm.shape),
            _const_spec(sbias.shape),
            _const_spec(w_out.shape[1:], lead=0),
            _const_spec(ln_g.shape),
            _const_spec(ln_b.shape),
        ] + cast_in,
        out_specs=[pl.BlockSpec((tm, d), lambda i: (i, 0))] + cast_out,
        out_shape=[jax.ShapeDtypeStruct((n, d), F32)] + cast_shapes,
        scratch_shapes=[pltpu.VMEM((GROUP, width), F32),
                        pltpu.VMEM(w_in.shape[1:], BF16),
                        pltpu.VMEM(w_out.shape[1:], BF16)],
        compiler_params=pltpu.CompilerParams(
            dimension_semantics=("arbitrary",), vmem_limit_bytes=VMEM_LIMIT_BYTES),
        name="conv_gmlp_mixer",
    )(*([x2d] * n_col_blocks), w_in, a_conv_w, a_conv_b, a_norm_g, a_norm_b, b_norm_g, b_norm_b,
      ws_perm, sbias, w_out, ln_g, ln_b, ffn_w_up, ffn_w_down)


def _attn_kernel(sinks_ref, x_ref, wqkv_f32_ref, bqkv_ref, wo_f32_ref, g_ref, b_ref,
                 ffn_up_ref, ffn_dn_ref, o_ref, ffn_up_bf_ref, ffn_dn_bf_ref,
                 attn_ref, kv_carry_ref, wqkv_ref, wo_ref, *, alpha, tiles_per_batch, sub_tile):
    _cast_weight_slabs(ffn_up_ref, ffn_dn_ref, ffn_up_bf_ref, ffn_dn_bf_ref)
    first = pl.program_id(0) % tiles_per_batch == 0

    @pl.when(first)
    def _():
        kv_carry_ref[...] = jnp.zeros_like(kv_carry_ref)

    @pl.when(pl.program_id(0) == 0)
    def _():
        wqkv_ref[...] = wqkv_f32_ref[...].astype(BF16)
        wo_ref[...] = wo_f32_ref[...].astype(BF16)

    for t in range(x_ref.shape[0] // sub_tile):
        _attn_rows(t * sub_tile, sub_tile, first if t == 0 else None, sinks_ref, x_ref, bqkv_ref,
                   g_ref, b_ref, o_ref, attn_ref, kv_carry_ref, wqkv_ref, wo_ref, alpha=alpha)


def _attn_rows(row0, tm, first, sinks_ref, x_ref, bqkv_ref, g_ref, b_ref, o_ref,
               attn_ref, kv_carry_ref, wqkv_ref, wo_ref, *, alpha):
    d_q = N_KV_HEADS * Q_PER_KV * HEAD_DIM
    d_kv = N_KV_HEADS * HEAD_DIM
    pair_w = 2 * HEAD_DIM
    pairs = Q_PER_KV // 2
    x = x_ref[row0:row0 + tm, :]
    qkv = _dot(x.astype(BF16), wqkv_ref[...]) + bqkv_ref[...]
    q = (qkv[:, :d_q] * (1.0 / math.sqrt(HEAD_DIM))).astype(BF16)
    kv = qkv[:, d_q:]
    kv_all = jnp.concatenate([kv_carry_ref[...], kv], axis=0)
    kv_carry_ref[...] = kv[tm - GROUP:]
    k_all = kv_all[:, :d_kv]
    v_all = kv_all[:, d_kv:]
    k_sw = pltpu.roll(k_all, HEAD_DIM, axis=1)
    v_sw = pltpu.roll(v_all, HEAD_DIM, axis=1)
    low = lax.broadcasted_iota(jnp.int32, k_all.shape, 1) < HEAD_DIM
    low_q = lax.broadcasted_iota(jnp.int32, (GROUP, pair_w), 1) < HEAD_DIM

    tq = _time_of_row(lax.broadcasted_iota(jnp.int32, (GROUP, GROUP), 0))
    tk = _time_of_row(lax.broadcasted_iota(jnp.int32, (GROUP, GROUP), 1))
    use_cur = tk <= tq
    prev_fill = None if first is None else jnp.where(first, -jnp.inf, 0.0).astype(F32)
    nt = (((1,), (1,)), ((), ()))

    for kvh in range(N_KV_HEADS):
        k_src, k_alt = (k_all, k_sw) if kvh == 0 else (k_sw, k_all)
        v_src, v_alt = (v_all, v_sw) if kvh == 0 else (v_sw, v_all)
        k_even = jnp.where(low, k_src, 0.0).astype(BF16)
        k_odd = jnp.where(low, 0.0, k_alt).astype(BF16)
        v_even = jnp.where(low, v_src, 0.0).astype(BF16)
        v_odd = jnp.where(low, 0.0, v_alt).astype(BF16)
        for n in range(tm // GROUP):
            kr = slice(n * GROUP, (n + 2) * GROUP)
            qr = slice(n * GROUP, (n + 1) * GROUP)
            keys = jnp.concatenate([k_even[kr], k_odd[kr]], axis=0)
            vals = jnp.concatenate([v_even[kr], v_odd[kr]], axis=0)
            col0 = kvh * Q_PER_KV * HEAD_DIM
            qs = jnp.concatenate(
                [q[qr, col0 + p * pair_w:col0 + (p + 1) * pair_w] for p in range(pairs)], axis=0)
            s = lax.dot_general(qs, keys, nt, preferred_element_type=F32)
            prob_rows, inv_den = [], []
            for p in range(pairs):
                halves, inv = [], []
                for e in range(2):
                    c0 = e * 2 * GROUP
                    s_prev = s[p * GROUP:(p + 1) * GROUP, c0:c0 + GROUP]
                    s_cur = s[p * GROUP:(p + 1) * GROUP, c0 + GROUP:c0 + 2 * GROUP]
                    if n == 0 and prev_fill is not None:
                        s_prev = s_prev + prev_fill
                    merged = jnp.where(use_cur, s_cur, s_prev)
                    sink = sinks_ref[0, kvh * Q_PER_KV + 2 * p + e]
                    m = jnp.maximum(jnp.max(merged, axis=1, keepdims=True), sink)
                    pe = jnp.exp(merged - m)
                    den = jnp.sum(pe, axis=1, keepdims=True) + jnp.exp(sink - m)
                    inv.append(1.0 / den)
                    halves += [jnp.where(use_cur, 0.0, pe).astype(BF16),
                               jnp.where(use_cur, pe, 0.0).astype(BF16)]
                prob_rows.append(jnp.concatenate(halves, axis=1))
                inv_den.append(jnp.where(low_q, inv[0], inv[1]))
            probs = jnp.concatenate(prob_rows, axis=0)
            o = _dot(probs, vals)
            for p in range(pairs):
                attn_ref[row0 + n * GROUP:row0 + (n + 1) * GROUP,
                         col0 + p * pair_w:col0 + (p + 1) * pair_w] = (
                    o[p * GROUP:(p + 1) * GROUP] * inv_den[p]).astype(BF16)

    mix = _dot(attn_ref[row0:row0 + tm, :], wo_ref[...])
    o_ref[row0:row0 + tm, :] = _layer_norm(alpha * x + mix, g_ref[1, 0:1, :], b_ref[1, 0:1, :])


def _attn_call(x2d, w_qkv, b_qkv, sinks, w_o, ln_g, ln_b, ffn_w_up, ffn_w_down, *, alpha, seq):
    n, d = x2d.shape
    d_q = w_o.shape[1]
    tm = ATTN_ROW_TILE
    kern = functools.partial(_attn_kernel, alpha=alpha, tiles_per_batch=seq // tm,
                             sub_tile=ATTN_SUB_TILE)
    cast_in, cast_out, cast_shapes = _weight_cast_stream(ffn_w_up, ffn_w_down, 1, n // tm)
    return pl.pallas_call(
        kern,
        grid=(n // tm,),
        in_specs=[
            pl.BlockSpec(memory_space=pltpu.SMEM),
            pl.BlockSpec((tm, d), lambda i: (i, 0)),
            _const_spec(w_qkv.shape[1:], lead=0),
            _const_spec(b_qkv.shape),
            _const_spec(w_o.shape[1:], lead=0),
            _const_spec(ln_g.shape),
            _const_spec(ln_b.shape),
        ] + cast_in,
        out_specs=[pl.BlockSpec((tm, d), lambda i: (i, 0))] + cast_out,
        out_shape=[jax.ShapeDtypeStruct((n, d), F32)] + cast_shapes,
        scratch_shapes=[
            pltpu.VMEM((tm, d_q), BF16),
            pltpu.VMEM((GROUP, 2 * N_KV_HEADS * HEAD_DIM), F32),
            pltpu.VMEM(w_qkv.shape[1:], BF16),
            pltpu.VMEM(w_o.shape[1:], BF16),
        ],
        compiler_params=pltpu.CompilerParams(
            dimension_semantics=("arbitrary",), vmem_limit_bytes=VMEM_LIMIT_BYTES),
        name="swa_sink_attention",
    )(sinks, x2d, w_qkv, b_qkv, w_o, ln_g, ln_b, ffn_w_up, ffn_w_down)


def kernel(x, ab_w_in, a_conv_w, a_conv_b, a_norm_g, a_norm_b, b_norm_g, b_norm_b, b_spatial_w, b_spatial_b, ab_w_out, c_w_qkv, c_b_qkv, c_sinks, c_w_o, ffn_w_up, ffn_conv_w, ffn_conv_b, ffn_w_down, ln_g, ln_b):
    bsz, seq, d = x.shape
    depth = ffn_w_up.shape[0]
    assert depth == 2 and all(seq % t == 0 for t in (FFN_ROW_TILE, ROW_TILE, ATTN_ROW_TILE))
    alpha = float(np.float32((2.0 * depth) ** 0.25))
    h = x.reshape(bsz * seq, d)
    h, w_up_bf, w_down_bf = _mixer_call(
        h, ab_w_in, a_conv_w, a_conv_b, a_norm_g, a_norm_b, b_norm_g, b_norm_b, b_spatial_w,
        b_spatial_b, ab_w_out, ln_g, ln_b, ffn_w_up, ffn_w_down, alpha=alpha, seq=seq)
    h = _ffn_call(h, w_up_bf, ffn_conv_w, ffn_conv_b, w_down_bf, ln_g, ln_b,
                  layer=0, alpha=alpha, seq=seq, natural_out=False)
    h, w_up_bf, w_down_bf = _attn_call(h, c_w_qkv, c_b_qkv, c_sinks, c_w_o, ln_g, ln_b,
                                       ffn_w_up, ffn_w_down, alpha=alpha, seq=seq)
    h = _ffn_call(h, w_up_bf, ffn_conv_w, ffn_conv_b, w_down_bf, ln_g, ln_b,
                  layer=1, alpha=alpha, seq=seq, natural_out=True)
    return h.reshape(bsz, seq, d)
```

```python
import functools
import math

import jax
import jax.numpy as jnp
import numpy as np
from jax import lax
from jax.experimental import pallas as pl
from jax.experimental.pallas import tpu as pltpu

F32 = jnp.float32
BF16 = jnp.bfloat16

LN_EPS = 1e-5
SUBLANES = 8
LANES = 128
BF16_ROWS = 16
GROUP = 128
VROWS = GROUP // SUBLANES
GMLP_GROUPS = 4
HEAD_DIM = 64
Q_PER_KV = 8
N_KV_HEADS = 2
VMEM_LIMIT_BYTES = 56 * 1024 * 1024

ROW_TILE = 512
ATTN_ROW_TILE = 1024
ATTN_SUB_TILE = 512
FFN_ROW_TILE = 1024
FFN_SUB_TILE = 512
FFN_COL_TILE = 256

GELU_C0 = math.sqrt(2.0 / math.pi)
GELU_C1 = 0.044715 * GELU_C0


def _layer_norm(z, g, b):
    mu = jnp.mean(z, axis=-1, keepdims=True)
    d = z - mu
    var = jnp.mean(d * d, axis=-1, keepdims=True)
    return d * lax.rsqrt(var + LN_EPS) * g + b


def _dot(a, b):
    return jnp.dot(a, b, preferred_element_type=F32)


def _one_plus_tanh_gelu_arg(x):
    return 1.0 + jnp.tanh(x * (GELU_C0 + GELU_C1 * (x * x)))


def _sigmoid(x):
    return 0.5 * (1.0 + jnp.tanh(0.5 * x))


def _time_of_row(rho):
    return VROWS * (rho & (SUBLANES - 1)) + (rho >> 3)


def _perm_index():
    rho = np.arange(GROUP)
    return VROWS * (rho % SUBLANES) + rho // SUBLANES


def _sublane_shift(cur, prev):
    row = lax.broadcasted_iota(jnp.int32, cur.shape, 0)
    return jnp.where(row == 0, pltpu.roll(prev, 1, axis=0), pltpu.roll(cur, 1, axis=0))


def _const_spec(shape, lead=None):
    nd = len(shape)
    if lead is None:
        return pl.BlockSpec(shape, lambda i: (0,) * nd, pipeline_mode=pl.Buffered(1))
    return pl.BlockSpec((None,) + tuple(shape), lambda i: (lead,) + (0,) * nd,
                        pipeline_mode=pl.Buffered(1))


def _weight_cast_stream(ffn_w_up, ffn_w_down, layer, n_steps):
    _, d, up_cols = ffn_w_up.shape
    _, d_ff, _ = ffn_w_down.shape
    assert d % (n_steps * BF16_ROWS) == 0
    up_rows = d // n_steps
    dn_blocks = max(nb for nb in range(1, n_steps + 1)
                    if d_ff % nb == 0 and (d_ff // nb) % BF16_ROWS == 0)
    dn_rows = d_ff // dn_blocks
    dn_index = lambda i: jnp.minimum(i, dn_blocks - 1)
    in_specs = [pl.BlockSpec((None, up_rows, up_cols), lambda i: (layer, i, 0)),
                pl.BlockSpec((None, dn_rows, d), lambda i: (layer, dn_index(i), 0))]
    out_specs = [pl.BlockSpec((up_rows, up_cols), lambda i: (i, 0)),
                 pl.BlockSpec((dn_rows, d), lambda i: (dn_index(i), 0))]
    out_shapes = [jax.ShapeDtypeStruct((d, up_cols), BF16), jax.ShapeDtypeStruct((d_ff, d), BF16)]
    return in_specs, out_specs, out_shapes


def _cast_weight_slabs(wup_in_ref, wdn_in_ref, wup_out_ref, wdn_out_ref):
    wup_out_ref[...] = wup_in_ref[...].astype(BF16)
    wdn_out_ref[...] = wdn_in_ref[...].astype(BF16)


def _ffn_rows(rows, x_ref, wup_ref, cw_ref, cb_ref, wdn_ref, g, b, o_ref, act_ref, carry_ref,
              stage_ref, *, alpha, layer, d_ff, col_tile):
    tm = rows.stop - rows.start
    n_groups = tm // GROUP
    x = x_ref[rows, :]
    xb = x.astype(BF16)

    def conv_chunk(col, scale):
        h = _dot(xb, wup_ref[:, col:col + col_tile])
        w = cw_ref[:, col:col + col_tile]
        bias = cb_ref[layer:layer + 1, col:col + col_tile]
        if scale != 1.0:
            w, bias = w * scale, bias * scale
        prev = [carry_ref[0:SUBLANES, col:col + col_tile],
                carry_ref[SUBLANES:2 * SUBLANES, col:col + col_tile]]
        h1, h2 = [], []
        for gi in range(n_groups):
            hg = h[gi * GROUP:(gi + 1) * GROUP]
            cur = [hg[GROUP - 2 * SUBLANES:GROUP - SUBLANES], hg[GROUP - SUBLANES:]]
            wrapped = [_sublane_shift(cur[i], prev[i]) for i in range(2)]
            h1 += [wrapped[1], hg[:GROUP - SUBLANES]]
            h2 += [wrapped[0], wrapped[1], hg[:GROUP - 2 * SUBLANES]]
            prev = cur
        carry_ref[0:SUBLANES, col:col + col_tile] = prev[0]
        carry_ref[SUBLANES:2 * SUBLANES, col:col + col_tile] = prev[1]
        y = w[2:3] * h + bias
        y = y + w[1:2] * jnp.concatenate(h1, axis=0)
        y = y + w[0:1] * jnp.concatenate(h2, axis=0)
        return y

    for c in range(d_ff // col_tile):
        gate = conv_chunk(c * col_tile, 1.0)
        half_val = conv_chunk(d_ff + c * col_tile, 0.5)
        act = gate * _one_plus_tanh_gelu_arg(gate) * half_val
        act_ref[rows, c * col_tile:(c + 1) * col_tile] = act.astype(BF16)

    ffn = _dot(act_ref[rows, :], wdn_ref[...])
    y = _layer_norm(alpha * x + ffn, g, b)
    if stage_ref is None:
        o_ref[rows, :] = y
    else:
        d = y.shape[1]
        for c in range(d // LANES):
            stage_ref[c] = y[:, c * LANES:(c + 1) * LANES]
        for c in range(d // LANES):
            for gi in range(n_groups):
                for s in range(SUBLANES):
                    for half in range(VROWS // SUBLANES):
                        src = gi * GROUP + half * SUBLANES * SUBLANES + s
                        dst = rows.start + gi * GROUP + VROWS * s + SUBLANES * half
                        o_ref[dst:dst + SUBLANES, c * LANES:(c + 1) * LANES] = (
                            stage_ref[c, pl.ds(src, SUBLANES, stride=SUBLANES), :])


def _ffn_kernel(x_ref, wup_ref, cw_ref, cb_ref, wdn_ref, g_ref, b_ref, o_ref,
                act_ref, carry_ref, *rest, tiles_per_batch, sub_tile, **static):
    tm = x_ref.shape[0]

    @pl.when(pl.program_id(0) % tiles_per_batch == 0)
    def _():
        carry_ref[...] = jnp.zeros_like(carry_ref)

    layer = static["layer"]
    g = g_ref[layer, 1:2, :]
    b = b_ref[layer, 1:2, :]
    stage_ref = rest[0] if rest else None
    for t in range(tm // sub_tile):
        _ffn_rows(slice(t * sub_tile, (t + 1) * sub_tile), x_ref, wup_ref, cw_ref, cb_ref, wdn_ref,
                  g, b, o_ref, act_ref, carry_ref, stage_ref, **static)


def _ffn_call(x2d, w_up_bf, conv_w, conv_b, w_down_bf, ln_g, ln_b, *, layer, alpha, seq,
              natural_out):
    n, d = x2d.shape
    d_ff = w_down_bf.shape[0]
    tm = FFN_ROW_TILE
    kern = functools.partial(_ffn_kernel, alpha=alpha, layer=layer, tiles_per_batch=seq // tm,
                             sub_tile=FFN_SUB_TILE, d_ff=d_ff, col_tile=FFN_COL_TILE)
    scratch = [pltpu.VMEM((tm, d_ff), BF16), pltpu.VMEM((2 * SUBLANES, 2 * d_ff), F32)]
    if natural_out:
        scratch.append(pltpu.VMEM((d // LANES, FFN_SUB_TILE, LANES), F32))
    return pl.pallas_call(
        kern,
        grid=(n // tm,),
        in_specs=[
            pl.BlockSpec((tm, d), lambda i: (i, 0)),
            _const_spec((d, 2 * d_ff)),
            _const_spec((3, 2 * d_ff), lead=layer),
            _const_spec(conv_b.shape),
            _const_spec((d_ff, d)),
            _const_spec(ln_g.shape),
            _const_spec(ln_b.shape),
        ],
        out_specs=pl.BlockSpec((tm, d), lambda i: (i, 0)),
        out_shape=jax.ShapeDtypeStruct((n, d), F32),
        scratch_shapes=scratch,
        compiler_params=pltpu.CompilerParams(
            dimension_semantics=("arbitrary",), vmem_limit_bytes=VMEM_LIMIT_BYTES),
        name="conv_ffn_out" if natural_out else "conv_ffn",
    )(x2d, w_up_bf, conv_w, conv_b, w_down_bf, ln_g, ln_b)


def _mixer_kernel(*refs, alpha, tiles_per_batch, width, n_col_blocks):
    x_refs = refs[:n_col_blocks]
    (win_f32_ref, acw_ref, acb_ref, ang_ref, anb_ref, bng_ref, bnb_ref,
     ws_ref, sb_ref, wout_f32_ref, g_ref, b_ref, ffn_up_ref, ffn_dn_ref,
     o_ref, ffn_up_bf_ref, ffn_dn_bf_ref, carry_ref, win_ref, wout_ref) = refs[n_col_blocks:]
    tm = o_ref.shape[0]
    n_groups = tm // GROUP
    _cast_weight_slabs(ffn_up_ref, ffn_dn_ref, ffn_up_bf_ref, ffn_dn_bf_ref)

    @pl.when(pl.program_id(0) == 0)
    def _():
        win_ref[...] = win_f32_ref[...].astype(BF16)
        wout_ref[...] = wout_f32_ref[...].astype(BF16)

    @pl.when(pl.program_id(0) % tiles_per_batch == 0)
    def _():
        carry_ref[...] = jnp.zeros_like(carry_ref)

    x = jnp.concatenate(
        [jnp.concatenate(
            [x_refs[c][pl.ds(g * GROUP + j, SUBLANES, stride=VROWS), :]
             for g in range(n_groups) for j in range(VROWS)], axis=0)
         for c in range(n_col_blocks)], axis=1)
    xb = x.astype(BF16)
    half_w = width // 2
    a_halves = []
    for hb in range(2):
        a_val = _dot(xb, win_ref[:, hb * half_w:(hb + 1) * half_w])
        a_gate = _dot(xb, win_ref[:, width + hb * half_w:width + (hb + 1) * half_w])
        a_halves.append(a_val * _sigmoid(a_gate))
    a = jnp.concatenate(a_halves, axis=1)
    b_u = _dot(xb, win_ref[:, 2 * width:3 * width])
    b_v = _dot(xb, win_ref[:, 3 * width:4 * width])
    ext = jnp.concatenate([carry_ref[...], a], axis=0)
    carry_ref[...] = a[tm - GROUP:]
    n_taps = acw_ref.shape[0]
    max_sub = (n_taps - 1) // VROWS + 1

    def sublane_shifted(z):
        out = []
        for g in range(n_groups + 1):
            zg = z[g * GROUP:(g + 1) * GROUP]
            zp = z[(g - 1) * GROUP:g * GROUP] if g > 0 else zg
            out += [_sublane_shift(zg[j * SUBLANES:(j + 1) * SUBLANES],
                                   zp[j * SUBLANES:(j + 1) * SUBLANES]) for j in range(VROWS)]
        return jnp.concatenate(out, axis=0)

    shifted = [ext]
    for _ in range(max_sub):
        shifted.append(sublane_shifted(shifted[-1]))
    acw = acw_ref[...]
    conv = jnp.broadcast_to(acb_ref[...], (tm, width))
    for k in range(n_taps):
        q, r = divmod(k, VROWS)
        lo = shifted[q][GROUP:]
        if r == 0:
            term = lo
        else:
            hi = shifted[q + 1][GROUP:]
            cut = (VROWS - r) * SUBLANES
            parts = []
            for g in range(n_groups):
                parts += [hi[g * GROUP + cut:(g + 1) * GROUP], lo[g * GROUP:g * GROUP + cut]]
            term = jnp.concatenate(parts, axis=0)
        j = n_taps - 1 - k
        conv = conv + acw[j:j + 1] * term
    a_out = _layer_norm(conv, ang_ref[...], anb_ref[...])
    a_out = a_out * _sigmoid(a_out)

    gelu_v = 0.5 * b_v * _one_plus_tanh_gelu_arg(b_v)
    v = _layer_norm(gelu_v, bng_ref[...], bnb_ref[...]).astype(BF16)
    ti = _time_of_row(lax.broadcasted_iota(jnp.int32, (GROUP, GROUP), 0))
    tj = _time_of_row(lax.broadcasted_iota(jnp.int32, (GROUP, GROUP), 1))
    gd = width // GMLP_GROUPS
    ws = [(jnp.where(ti >= tj, ws_ref[g], 0.0) * 0.5).astype(BF16) for g in range(GMLP_GROUPS)]
    half_bias = sb_ref[...] * 0.5
    chunks = []
    for c in range(n_groups):
        rows = slice(c * GROUP, (c + 1) * GROUP)
        cols = [_dot(ws[g], v[rows, g * gd:(g + 1) * gd]) for g in range(GMLP_GROUPS)]
        chunks.append(jnp.concatenate(cols, axis=1) + half_bias)
    b_out = b_u * _one_plus_tanh_gelu_arg(b_u) * jnp.concatenate(chunks, axis=0)

    mix = _dot(jnp.concatenate([a_out.astype(BF16), b_out.astype(BF16)], axis=1), wout_ref[...])
    o_ref[...] = _layer_norm(alpha * x + mix, g_ref[0, 0:1, :], b_ref[0, 0:1, :])


def _mixer_call(x2d, w_in, a_conv_w, a_conv_b, a_norm_g, a_norm_b, b_norm_g, b_norm_b,
                b_spatial_w, b_spatial_b, w_out, ln_g, ln_b, ffn_w_up, ffn_w_down, *, alpha, seq):
    n, d = x2d.shape
    width = a_conv_w.shape[2]
    gd = width // GMLP_GROUPS
    tm = ROW_TILE
    n_col_blocks = d // LANES
    perm = _perm_index()
    ws_perm = b_spatial_w[0][:, perm][:, :, perm]
    sbias = jnp.repeat(b_spatial_b[0].T[perm], gd, axis=1)
    kern = functools.partial(_mixer_kernel, alpha=alpha, tiles_per_batch=seq // tm, width=width,
                             n_col_blocks=n_col_blocks)
    x_specs = [pl.BlockSpec((tm, LANES), functools.partial(lambda c, i: (i, c), c))
               for c in range(n_col_blocks)]
    cast_in, cast_out, cast_shapes = _weight_cast_stream(ffn_w_up, ffn_w_down, 0, n // tm)
    return pl.pallas_call(
        kern,
        grid=(n // tm,),
        in_specs=x_specs + [
            _const_spec(w_in.shape[1:], lead=0),
            _const_spec(a_conv_w.shape[1:], lead=0),
            _const_spec(a_conv_b.shape),
            _const_spec(a_norm_g.shape),
            _const_spec(a_norm_b.shape),
            _const_spec(b_norm_g.shape),
            _const_spec(b_norm_b.shape),
            _const_spec(ws_perm.shape),
            _const_spec(sbias.shape),
            _const_spec(w_out.shape[1:], lead=0),
            _const_spec(ln_g.shape),
            _const_spec(ln_b.shape),
        ] + cast_in,
        out_specs=[pl.BlockSpec((tm, d), lambda i: (i, 0))] + cast_out,
        out_shape=[jax.ShapeDtypeStruct((n, d), F32)] + cast_shapes,
        scratch_shapes=[pltpu.VMEM((GROUP, width), F32),
                        pltpu.VMEM(w_in.shape[1:], BF16),
                        pltpu.VMEM(w_out.shape[1:], BF16)],
        compiler_params=pltpu.CompilerParams(
            dimension_semantics=("arbitrary",), vmem_limit_bytes=VMEM_LIMIT_BYTES),
        name="conv_gmlp_mixer",
    )(*([x2d] * n_col_blocks), w_in, a_conv_w, a_conv_b, a_norm_g, a_norm_b, b_norm_g, b_norm_b,
      ws_perm, sbias, w_out, ln_g, ln_b, ffn_w_up, ffn_w_down)


def _attn_kernel(sinks_ref, x_ref, wqkv_f32_ref, bqkv_ref, wo_f32_ref, g_ref, b_ref,
                 ffn_up_ref, ffn_dn_ref, o_ref, ffn_up_bf_ref, ffn_dn_bf_ref,
                 attn_ref, kv_carry_ref, wqkv_ref, wo_ref, *, alpha, tiles_per_batch, sub_tile):
    _cast_weight_slabs(ffn_up_ref, ffn_dn_ref, ffn_up_bf_ref, ffn_dn_bf_ref)
    first = pl.program_id(0) % tiles_per_batch == 0

    @pl.when(first)
    def _():
        kv_carry_ref[...] = jnp.zeros_like(kv_carry_ref)

    @pl.when(pl.program_id(0) == 0)
    def _():
        wqkv_ref[...] = wqkv_f32_ref[...].astype(BF16)
        wo_ref[...] = wo_f32_ref[...].astype(BF16)

    for t in range(x_ref.shape[0] // sub_tile):
        _attn_rows(t * sub_tile, sub_tile, first if t == 0 else None, sinks_ref, x_ref, bqkv_ref,
                   g_ref, b_ref, o_ref, attn_ref, kv_carry_ref, wqkv_ref, wo_ref, alpha=alpha)


def _attn_rows(row0, tm, first, sinks_ref, x_ref, bqkv_ref, g_ref, b_ref, o_ref,
               attn_ref, kv_carry_ref, wqkv_ref, wo_ref, *, alpha):
    d_q = N_KV_HEADS * Q_PER_KV * HEAD_DIM
    d_kv = N_KV_HEADS * HEAD_DIM
    pair_w = 2 * HEAD_DIM
    pairs = Q_PER_KV // 2
    x = x_ref[row0:row0 + tm, :]
    xb = x.astype(BF16)
    kv = _dot(xb, wqkv_ref[:, d_q:]) + bqkv_ref[:, d_q:]
    q_w = d_q // N_KV_HEADS
    q = jnp.concatenate(
        [((_dot(xb, wqkv_ref[:, h * q_w:(h + 1) * q_w]) + bqkv_ref[:, h * q_w:(h + 1) * q_w])
          * (1.0 / math.sqrt(HEAD_DIM))).astype(BF16) for h in range(N_KV_HEADS)], axis=1)
    kv_all = jnp.concatenate([kv_carry_ref[...], kv], axis=0)
    kv_carry_ref[...] = kv[tm - GROUP:]
    k_all = kv_all[:, :d_kv]
    v_all = kv_all[:, d_kv:]
    k_sw = pltpu.roll(k_all, HEAD_DIM, axis=1)
    v_sw = pltpu.roll(v_all, HEAD_DIM, axis=1)
    low = lax.broadcasted_iota(jnp.int32, k_all.shape, 1) < HEAD_DIM
    low_q = lax.broadcasted_iota(jnp.int32, (GROUP, pair_w), 1) < HEAD_DIM

    tq = _time_of_row(lax.broadcasted_iota(jnp.int32, (GROUP, GROUP), 0))
    tk = _time_of_row(lax.broadcasted_iota(jnp.int32, (GROUP, GROUP), 1))
    use_cur = tk <= tq
    prev_fill = None if first is None else jnp.where(first, -jnp.inf, 0.0).astype(F32)
    nt = (((1,), (1,)), ((), ()))

    for kvh in range(N_KV_HEADS):
        k_src, k_alt = (k_all, k_sw) if kvh == 0 else (k_sw, k_all)
        v_src, v_alt = (v_all, v_sw) if kvh == 0 else (v_sw, v_all)
        k_even = jnp.where(low, k_src, 0.0).astype(BF16)
        k_odd = jnp.where(low, 0.0, k_alt).astype(BF16)
        v_even = jnp.where(low, v_src, 0.0).astype(BF16)
        v_odd = jnp.where(low, 0.0, v_alt).astype(BF16)
        for n in range(tm // GROUP):
            kr = slice(n * GROUP, (n + 2) * GROUP)
            qr = slice(n * GROUP, (n + 1) * GROUP)
            keys = jnp.concatenate([k_even[kr], k_odd[kr]], axis=0)
            vals = jnp.concatenate([v_even[kr], v_odd[kr]], axis=0)
            col0 = kvh * Q_PER_KV * HEAD_DIM
            qs = jnp.concatenate(
                [q[qr, col0 + p * pair_w:col0 + (p + 1) * pair_w] for p in range(pairs)], axis=0)
            s = lax.dot_general(qs, keys, nt, preferred_element_type=F32)
            prob_rows, inv_den = [], []
            for p in range(pairs):
                halves, inv = [], []
                for e in range(2):
                    c0 = e * 2 * GROUP
                    s_prev = s[p * GROUP:(p + 1) * GROUP, c0:c0 + GROUP]
                    s_cur = s[p * GROUP:(p + 1) * GROUP, c0 + GROUP:c0 + 2 * GROUP]
                    if n == 0 and prev_fill is not None:
                        s_prev = s_prev + prev_fill
                    merged = jnp.where(use_cur, s_cur, s_prev)
                    sink = sinks_ref[0, kvh * Q_PER_KV + 2 * p + e]
                    m = jnp.maximum(jnp.max(merged, axis=1, keepdims=True), sink)
                    pe = jnp.exp(merged - m)
                    den = jnp.sum(pe, axis=1, keepdims=True) + jnp.exp(sink - m)
                    inv.append(1.0 / den)
                    halves += [jnp.where(use_cur, 0.0, pe).astype(BF16),
                               jnp.where(use_cur, pe, 0.0).astype(BF16)]
                prob_rows.append(jnp.concatenate(halves, axis=1))
                inv_den.append(jnp.where(low_q, inv[0], inv[1]))
            probs = jnp.concatenate(prob_rows, axis=0)
            o = _dot(probs, vals)
            for p in range(pairs):
                attn_ref[row0 + n * GROUP:row0 + (n + 1) * GROUP,
                         col0 + p * pair_w:col0 + (p + 1) * pair_w] = (
                    o[p * GROUP:(p + 1) * GROUP] * inv_den[p]).astype(BF16)

    mix = _dot(attn_ref[row0:row0 + tm, :], wo_ref[...])
    o_ref[row0:row0 + tm, :] = _layer_norm(alpha * x + mix, g_ref[1, 0:1, :], b_ref[1, 0:1, :])


def _attn_call(x2d, w_qkv, b_qkv, sinks, w_o, ln_g, ln_b, ffn_w_up, ffn_w_down, *, alpha, seq):
    n, d = x2d.shape
    d_q = w_o.shape[1]
    tm = ATTN_ROW_TILE
    kern = functools.partial(_attn_kernel, alpha=alpha, tiles_per_batch=seq // tm,
                             sub_tile=ATTN_SUB_TILE)
    cast_in, cast_out, cast_shapes = _weight_cast_stream(ffn_w_up, ffn_w_down, 1, n // tm)
    return pl.pallas_call(
        kern,
        grid=(n // tm,),
        in_specs=[
            pl.BlockSpec(memory_space=pltpu.SMEM),
            pl.BlockSpec((tm, d), lambda i: (i, 0)),
            _const_spec(w_qkv.shape[1:], lead=0),
            _const_spec(b_qkv.shape),
            _const_spec(w_o.shape[1:], lead=0),
            _const_spec(ln_g.shape),
            _const_spec(ln_b.shape),
        ] + cast_in,
        out_specs=[pl.BlockSpec((tm, d), lambda i: (i, 0))] + cast_out,
        out_shape=[jax.ShapeDtypeStruct((n, d), F32)] + cast_shapes,
        scratch_shapes=[
            pltpu.VMEM((tm, d_q), BF16),
            pltpu.VMEM((GROUP, 2 * N_KV_HEADS * HEAD_DIM), F32),
            pltpu.VMEM(w_qkv.shape[1:], BF16),
            pltpu.VMEM(w_o.shape[1:], BF16),
        ],
        compiler_params=pltpu.CompilerParams(
            dimension_semantics=("arbitrary",), vmem_limit_bytes=VMEM_LIMIT_BYTES),
        name="swa_sink_attention",
    )(sinks, x2d, w_qkv, b_qkv, w_o, ln_g, ln_b, ffn_w_up, ffn_w_down)


def kernel(x, ab_w_in, a_conv_w, a_conv_b, a_norm_g, a_norm_b, b_norm_g, b_norm_b, b_spatial_w, b_spatial_b, ab_w_out, c_w_qkv, c_b_qkv, c_sinks, c_w_o, ffn_w_up, ffn_conv_w, ffn_conv_b, ffn_w_down, ln_g, ln_b):
    bsz, seq, d = x.shape
    depth = ffn_w_up.shape[0]
    assert depth == 2 and all(seq % t == 0 for t in (FFN_ROW_TILE, ROW_TILE, ATTN_ROW_TILE))
    alpha = float(np.float32((2.0 * depth) ** 0.25))
    h = x.reshape(bsz * seq, d)
    h, w_up_bf, w_down_bf = _mixer_call(
        h, ab_w_in, a_conv_w, a_conv_b, a_norm_g, a_norm_b, b_norm_g, b_norm_b, b_spatial_w,
        b_spatial_b, ab_w_out, ln_g, ln_b, ffn_w_up, ffn_w_down, alpha=alpha, seq=seq)
    h = _ffn_call(h, w_up_bf, ffn_conv_w, ffn_conv_b, w_down_bf, ln_g, ln_b,
                  layer=0, alpha=alpha, seq=seq, natural_out=False)
    h, w_up_bf, w_down_bf = _attn_call(h, c_w_qkv, c_b_qkv, c_sinks, c_w_o, ln_g, ln_b,
                                       ffn_w_up, ffn_w_down, alpha=alpha, seq=seq)
    h = _ffn_call(h, w_up_bf, ffn_conv_w, ffn_conv_b, w_down_bf, ln_g, ln_b,
                  layer=1, alpha=alpha, seq=seq, natural_out=True)
    return h.reshape(bsz, seq, d)
```

```python
import functools
import math

import jax
import jax.numpy as jnp
import numpy as np
from jax import lax
from jax.experimental import pallas as pl
from jax.experimental.pallas import tpu as pltpu

F32 = jnp.float32
BF16 = jnp.bfloat16

LN_EPS = 1e-5
SUBLANES = 8
LANES = 128
BF16_ROWS = 16
GROUP = 128
VROWS = GROUP // SUBLANES
GMLP_GROUPS = 4
HEAD_DIM = 64
Q_PER_KV = 8
N_KV_HEADS = 2
VMEM_LIMIT_BYTES = 56 * 1024 * 1024

ROW_TILE = 512
ATTN_ROW_TILE = 1024
ATTN_SUB_TILE = 512
FFN_ROW_TILE = 1024
FFN_SUB_TILE = 512
FFN_COL_TILE = 256

GELU_C0 = math.sqrt(2.0 / math.pi)
GELU_C1 = 0.044715 * GELU_C0


def _layer_norm(z, g, b):
    mu = jnp.mean(z, axis=-1, keepdims=True)
    d = z - mu
    var = jnp.mean(d * d, axis=-1, keepdims=True)
    return d * lax.rsqrt(var + LN_EPS) * g + b


def _dot(a, b):
    return jnp.dot(a, b, preferred_element_type=F32)


def _one_plus_tanh_gelu_arg(x):
    return 1.0 + jnp.tanh(x * (GELU_C0 + GELU_C1 * (x * x)))


def _sigmoid(x):
    return 0.5 * (1.0 + jnp.tanh(0.5 * x))


def _time_of_row(rho):
    return VROWS * (rho & (SUBLANES - 1)) + (rho >> 3)


def _perm_index():
    rho = np.arange(GROUP)
    return VROWS * (rho % SUBLANES) + rho // SUBLANES


def _sublane_shift(cur, prev):
    row = lax.broadcasted_iota(jnp.int32, cur.shape, 0)
    return jnp.where(row == 0, pltpu.roll(prev, 1, axis=0), pltpu.roll(cur, 1, axis=0))


def _const_spec(shape, lead=None):
    nd = len(shape)
    if lead is None:
        return pl.BlockSpec(shape, lambda i: (0,) * nd, pipeline_mode=pl.Buffered(1))
    return pl.BlockSpec((None,) + tuple(shape), lambda i: (lead,) + (0,) * nd,
                        pipeline_mode=pl.Buffered(1))


def _weight_cast_stream(ffn_w_up, ffn_w_down, layer, n_steps):
    _, d, up_cols = ffn_w_up.shape
    _, d_ff, _ = ffn_w_down.shape
    assert d % (n_steps * BF16_ROWS) == 0
    up_rows = d // n_steps
    dn_blocks = max(nb for nb in range(1, n_steps + 1)
                    if d_ff % nb == 0 and (d_ff // nb) % BF16_ROWS == 0)
    dn_rows = d_ff // dn_blocks
    dn_index = lambda i: jnp.minimum(i, dn_blocks - 1)
    in_specs = [pl.BlockSpec((None, up_rows, up_cols), lambda i: (layer, i, 0)),
                pl.BlockSpec((None, dn_rows, d), lambda i: (layer, dn_index(i), 0))]
    out_specs = [pl.BlockSpec((up_rows, up_cols), lambda i: (i, 0)),
                 pl.BlockSpec((dn_rows, d), lambda i: (dn_index(i), 0))]
    out_shapes = [jax.ShapeDtypeStruct((d, up_cols), BF16), jax.ShapeDtypeStruct((d_ff, d), BF16)]
    return in_specs, out_specs, out_shapes


def _cast_weight_slabs(wup_in_ref, wdn_in_ref, wup_out_ref, wdn_out_ref):
    wup_out_ref[...] = wup_in_ref[...].astype(BF16)
    wdn_out_ref[...] = wdn_in_ref[...].astype(BF16)


def _ffn_rows(rows, x_ref, wup_ref, cw_ref, cb_ref, wdn_ref, g, b, o_ref, act_ref, carry_ref,
              stage_ref, *, alpha, layer, d_ff, col_tile):
    tm = rows.stop - rows.start
    n_groups = tm // GROUP
    x = x_ref[rows, :]
    xb = x.astype(BF16)

    def conv_chunk(col, scale):
        h = _dot(xb, wup_ref[:, col:col + col_tile])
        w = cw_ref[:, col:col + col_tile]
        bias = cb_ref[layer:layer + 1, col:col + col_tile]
        if scale != 1.0:
            w, bias = w * scale, bias * scale
        prev = [carry_ref[0:SUBLANES, col:col + col_tile],
                carry_ref[SUBLANES:2 * SUBLANES, col:col + col_tile]]
        h1, h2 = [], []
        for gi in range(n_groups):
            hg = h[gi * GROUP:(gi + 1) * GROUP]
            cur = [hg[GROUP - 2 * SUBLANES:GROUP - SUBLANES], hg[GROUP - SUBLANES:]]
            wrapped = [_sublane_shift(cur[i], prev[i]) for i in range(2)]
            h1 += [wrapped[1], hg[:GROUP - SUBLANES]]
            h2 += [wrapped[0], wrapped[1], hg[:GROUP - 2 * SUBLANES]]
            prev = cur
        carry_ref[0:SUBLANES, col:col + col_tile] = prev[0]
        carry_ref[SUBLANES:2 * SUBLANES, col:col + col_tile] = prev[1]
        y = w[2:3] * h + bias
        y = y + w[1:2] * jnp.concatenate(h1, axis=0)
        y = y + w[0:1] * jnp.concatenate(h2, axis=0)
        return y

    for c in range(d_ff // col_tile):
        gate = conv_chunk(c * col_tile, 1.0)
        half_val = conv_chunk(d_ff + c * col_tile, 0.5)
        act = gate * _one_plus_tanh_gelu_arg(gate) * half_val
        act_ref[rows, c * col_tile:(c + 1) * col_tile] = act.astype(BF16)

    ffn = _dot(act_ref[rows, :], wdn_ref[...])
    y = _layer_norm(alpha * x + ffn, g, b)
    if stage_ref is None:
        o_ref[rows, :] = y
    else:
        d = y.shape[1]
        for c in range(d // LANES):
            stage_ref[c] = y[:, c * LANES:(c + 1) * LANES]
        for c in range(d // LANES):
            for gi in range(n_groups):
                for s in range(SUBLANES):
                    for half in range(VROWS // SUBLANES):
                        src = gi * GROUP + half * SUBLANES * SUBLANES + s
                        dst = rows.start + gi * GROUP + VROWS * s + SUBLANES * half
                        o_ref[dst:dst + SUBLANES, c * LANES:(c + 1) * LANES] = (
                            stage_ref[c, pl.ds(src, SUBLANES, stride=SUBLANES), :])


def _ffn_kernel(x_ref, wup_ref, cw_ref, cb_ref, wdn_ref, g_ref, b_ref, o_ref,
                act_ref, carry_ref, *rest, tiles_per_batch, sub_tile, **static):
    tm = x_ref.shape[0]

    @pl.when(pl.program_id(0) % tiles_per_batch == 0)
    def _():
        carry_ref[...] = jnp.zeros_like(carry_ref)

    layer = static["layer"]
    g = g_ref[layer, 1:2, :]
    b = b_ref[layer, 1:2, :]
    stage_ref = rest[0] if rest else None
    for t in range(tm // sub_tile):
        _ffn_rows(slice(t * sub_tile, (t + 1) * sub_tile), x_ref, wup_ref, cw_ref, cb_ref, wdn_ref,
                  g, b, o_ref, act_ref, carry_ref, stage_ref, **static)


def _ffn_call(x2d, w_up_bf, conv_w, conv_b, w_down_bf, ln_g, ln_b, *, layer, alpha, seq,
              natural_out):
    n, d = x2d.shape
    d_ff = w_down_bf.shape[0]
    tm = FFN_ROW_TILE
    kern = functools.partial(_ffn_kernel, alpha=alpha, layer=layer, tiles_per_batch=seq // tm,
                             sub_tile=FFN_SUB_TILE, d_ff=d_ff, col_tile=FFN_COL_TILE)
    scratch = [pltpu.VMEM((tm, d_ff), BF16), pltpu.VMEM((2 * SUBLANES, 2 * d_ff), F32)]
    if natural_out:
        scratch.append(pltpu.VMEM((d // LANES, FFN_SUB_TILE, LANES), F32))
    return pl.pallas_call(
        kern,
        grid=(n // tm,),
        in_specs=[
            pl.BlockSpec((tm, d), lambda i: (i, 0)),
            _const_spec((d, 2 * d_ff)),
            _const_spec((3, 2 * d_ff), lead=layer),
            _const_spec(conv_b.shape),
            _const_spec((d_ff, d)),
            _const_spec(ln_g.shape),
            _const_spec(ln_b.shape),
        ],
        out_specs=pl.BlockSpec((tm, d), lambda i: (i, 0)),
        out_shape=jax.ShapeDtypeStruct((n, d), F32),
        scratch_shapes=scratch,
        compiler_params=pltpu.CompilerParams(
            dimension_semantics=("arbitrary",), vmem_limit_bytes=VMEM_LIMIT_BYTES),
        name="conv_ffn_out" if natural_out else "conv_ffn",
    )(x2d, w_up_bf, conv_w, conv_b, w_down_bf, ln_g, ln_b)


def _mixer_kernel(*refs, alpha, tiles_per_batch, width, n_col_blocks):
    x_refs = refs[:n_col_blocks]
    (win_f32_ref, acw_ref, acb_ref, ang_ref, anb_ref, bng_ref, bnb_ref,
     ws_ref, sb_ref, wout_f32_ref, g_ref, b_ref, ffn_up_ref, ffn_dn_ref,
     o_ref, ffn_up_bf_ref, ffn_dn_bf_ref, carry_ref, win_ref, wout_ref) = refs[n_col_blocks:]
    tm = o_ref.shape[0]
    n_groups = tm // GROUP
    _cast_weight_slabs(ffn_up_ref, ffn_dn_ref, ffn_up_bf_ref, ffn_dn_bf_ref)

    @pl.when(pl.program_id(0) == 0)
    def _():
        win_ref[...] = win_f32_ref[...].astype(BF16)
        wout_ref[...] = wout_f32_ref[...].astype(BF16)

    @pl.when(pl.program_id(0) % tiles_per_batch == 0)
    def _():
        carry_ref[...] = jnp.zeros_like(carry_ref)

    x = jnp.concatenate(
        [jnp.concatenate(
            [x_refs[c][pl.ds(g * GROUP + j, SUBLANES, stride=VROWS), :]
             for g in range(n_groups) for j in range(VROWS)], axis=0)
         for c in range(n_col_blocks)], axis=1)
    xb = x.astype(BF16)
    half_w = width // 2
    a_halves = []
    for hb in range(2):
        a_val = _dot(xb, win_ref[:, hb * half_w:(hb + 1) * half_w])
        a_gate = _dot(xb, win_ref[:, width + hb * half_w:width + (hb + 1) * half_w])
        a_halves.append(a_val * _sigmoid(a_gate))
    a = jnp.concatenate(a_halves, axis=1)
    b_v = _dot(xb, win_ref[:, 3 * width:4 * width])
    b_u = _dot(xb, win_ref[:, 2 * width:3 * width])
    ext = jnp.concatenate([carry_ref[...], a], axis=0)
    carry_ref[...] = a[tm - GROUP:]
    n_taps = acw_ref.shape[0]
    max_sub = (n_taps - 1) // VROWS + 1

    def sublane_shifted(z):
        out = []
        for g in range(n_groups + 1):
            zg = z[g * GROUP:(g + 1) * GROUP]
            zp = z[(g - 1) * GROUP:g * GROUP] if g > 0 else zg
            out += [_sublane_shift(zg[j * SUBLANES:(j + 1) * SUBLANES],
                                   zp[j * SUBLANES:(j + 1) * SUBLANES]) for j in range(VROWS)]
        return jnp.concatenate(out, axis=0)

    shifted = [ext]
    for _ in range(max_sub):
        shifted.append(sublane_shifted(shifted[-1]))
    acw = acw_ref[...]
    conv = jnp.broadcast_to(acb_ref[...], (tm, width))
    for k in range(n_taps):
        q, r = divmod(k, VROWS)
        lo = shifted[q][GROUP:]
        if r == 0:
            term = lo
        else:
            hi = shifted[q + 1][GROUP:]
            cut = (VROWS - r) * SUBLANES
            parts = []
            for g in range(n_groups):
                parts += [hi[g * GROUP + cut:(g + 1) * GROUP], lo[g * GROUP:g * GROUP + cut]]
            term = jnp.concatenate(parts, axis=0)
        j = n_taps - 1 - k
        conv = conv + acw[j:j + 1] * term
    a_out = _layer_norm(conv, ang_ref[...], anb_ref[...])
    a_out = a_out * _sigmoid(a_out)

    gelu_v = 0.5 * b_v * _one_plus_tanh_gelu_arg(b_v)
    v = _layer_norm(gelu_v, bng_ref[...], bnb_ref[...]).astype(BF16)
    ti = _time_of_row(lax.broadcasted_iota(jnp.int32, (GROUP, GROUP), 0))
    tj = _time_of_row(lax.broadcasted_iota(jnp.int32, (GROUP, GROUP), 1))
    gd = width // GMLP_GROUPS
    ws = [(jnp.where(ti >= tj, ws_ref[g], 0.0) * 0.5).astype(BF16) for g in range(GMLP_GROUPS)]
    half_bias = sb_ref[...] * 0.5
    chunks = []
    for c in range(n_groups):
        rows = slice(c * GROUP, (c + 1) * GROUP)
        cols = [_dot(ws[g], v[rows, g * gd:(g + 1) * gd]) for g in range(GMLP_GROUPS)]
        chunks.append(jnp.concatenate(cols, axis=1) + half_bias)
    b_out = b_u * _one_plus_tanh_gelu_arg(b_u) * jnp.concatenate(chunks, axis=0)

    mix = _dot(jnp.concatenate([a_out.astype(BF16), b_out.astype(BF16)], axis=1), wout_ref[...])
    o_ref[...] = _layer_norm(alpha * x + mix, g_ref[0, 0:1, :], b_ref[0, 0:1, :])


def _mixer_call(x2d, w_in, a_conv_w, a_conv_b, a_norm_g, a_norm_b, b_norm_g, b_norm_b,
                b_spatial_w, b_spatial_b, w_out, ln_g, ln_b, ffn_w_up, ffn_w_down, *, alpha, seq):
    n, d = x2d.shape
    width = a_conv_w.shape[2]
    gd = width // GMLP_GROUPS
    tm = ROW_TILE
    n_col_blocks = d // LANES
    perm = _perm_index()
    ws_perm = b_spatial_w[0][:, perm][:, :, perm]
    sbias = jnp.repeat(b_spatial_b[0].T[perm], gd, axis=1)
    kern = functools.partial(_mixer_kernel, alpha=alpha, tiles_per_batch=seq // tm, width=width,
                             n_col_blocks=n_col_blocks)
    x_specs = [pl.BlockSpec((tm, LANES), functools.partial(lambda c, i: (i, c), c))
               for c in range(n_col_blocks)]
    cast_in, cast_out, cast_shapes = _weight_cast_stream(ffn_w_up, ffn_w_down, 0, n // tm)
    return pl.pallas_call(
        kern,
        grid=(n // tm,),
        in_specs=x_specs + [
            _const_spec(w_in.shape[1:], lead=0),
            _const_spec(a_conv_w.shape[1:], lead=0),
            _const_spec(a_conv_b.shape),
            _const_spec(a_norm_g.shape),
            _const_spec(a_norm_b.shape),
            _const_spec(b_norm_g.shape),
            _const_spec(b_norm_b.shape),
            _const_spec(ws_perm.shape),
            _const_spec(sbias.shape),
            _const_spec(w_out.shape[1:], lead=0),
            _const_spec(ln_g.shape),
            _const_spec(ln_b.shape),
        ] + cast_in,
        out_specs=[pl.BlockSpec((tm, d), lambda i: (i, 0))] + cast_out,
        out_shape=[jax.ShapeDtypeStruct((n, d), F32)] + cast_shapes,
        scratch_shapes=[pltpu.VMEM((GROUP, width), F32),
                        pltpu.VMEM(w_in.shape[1:], BF16),
                        pltpu.VMEM(w_out.shape[1:], BF16)],
        compiler_params=pltpu.CompilerParams(
            dimension_semantics=("arbitrary",), vmem_limit_bytes=VMEM_LIMIT_BYTES),
        name="conv_gmlp_mixer",
    )(*([x2d] * n_col_blocks), w_in, a_conv_w, a_conv_b, a_norm_g, a_norm_b, b_norm_g, b_norm_b,
      ws_perm, sbias, w_out, ln_g, ln_b, ffn_w_up, ffn_w_down)


def _attn_kernel(sinks_ref, x_ref, wqkv_f32_ref, bqkv_ref, wo_f32_ref, g_ref, b_ref,
                 ffn_up_ref, ffn_dn_ref, o_ref, ffn_up_bf_ref, ffn_dn_bf_ref,
                 attn_ref, kv_carry_ref, wqkv_ref, wo_ref, *, alpha, tiles_per_batch, sub_tile):
    _cast_weight_slabs(ffn_up_ref, ffn_dn_ref, ffn_up_bf_ref, ffn_dn_bf_ref)
    first = pl.program_id(0) % tiles_per_batch == 0

    @pl.when(first)
    def _():
        kv_carry_ref[...] = jnp.zeros_like(kv_carry_ref)

    @pl.when(pl.program_id(0) == 0)
    def _():
        wqkv_ref[...] = wqkv_f32_ref[...].astype(BF16)
        wo_ref[...] = wo_f32_ref[...].astype(BF16)

    for t in range(x_ref.shape[0] // sub_tile):
        _attn_rows(t * sub_tile, sub_tile, first if t == 0 else None, sinks_ref, x_ref, bqkv_ref,
                   g_ref, b_ref, o_ref, attn_ref, kv_carry_ref, wqkv_ref, wo_ref, alpha=alpha)


def _attn_rows(row0, tm, first, sinks_ref, x_ref, bqkv_ref, g_ref, b_ref, o_ref,
               attn_ref, kv_carry_ref, wqkv_ref, wo_ref, *, alpha):
    d_q = N_KV_HEADS * Q_PER_KV * HEAD_DIM
    d_kv = N_KV_HEADS * HEAD_DIM
    pair_w = 2 * HEAD_DIM
    pairs = Q_PER_KV // 2
    x = x_ref[row0:row0 + tm, :]
    xb = x.astype(BF16)
    kv = _dot(xb, wqkv_ref[:, d_q:]) + bqkv_ref[:, d_q:]
    q_w = d_q // N_KV_HEADS
    q = jnp.concatenate(
        [((_dot(xb, wqkv_ref[:, h * q_w:(h + 1) * q_w]) + bqkv_ref[:, h * q_w:(h + 1) * q_w])
          * (1.0 / math.sqrt(HEAD_DIM))).astype(BF16) for h in range(N_KV_HEADS)], axis=1)
    kv_all = jnp.concatenate([kv_carry_ref[...], kv], axis=0)
    kv_carry_ref[...] = kv[tm - GROUP:]
    k_all = kv_all[:, :d_kv]
    v_all = kv_all[:, d_kv:]
    k_sw = pltpu.roll(k_all, HEAD_DIM, axis=1)
    v_sw = pltpu.roll(v_all, HEAD_DIM, axis=1)
    low = lax.broadcasted_iota(jnp.int32, k_all.shape, 1) < HEAD_DIM
    low_q = lax.broadcasted_iota(jnp.int32, (GROUP, pair_w), 1) < HEAD_DIM

    tq = _time_of_row(lax.broadcasted_iota(jnp.int32, (GROUP, GROUP), 0))
    tk = _time_of_row(lax.broadcasted_iota(jnp.int32, (GROUP, GROUP), 1))
    use_cur = tk <= tq
    prev_fill = None if first is None else jnp.where(first, -jnp.inf, 0.0).astype(F32)
    nt = (((1,), (1,)), ((), ()))

    for kvh in range(N_KV_HEADS):
        k_src, k_alt = (k_all, k_sw) if kvh == 0 else (k_sw, k_all)
        v_src, v_alt = (v_all, v_sw) if kvh == 0 else (v_sw, v_all)
        k_even = jnp.where(low, k_src, 0.0).astype(BF16)
        k_odd = jnp.where(low, 0.0, k_alt).astype(BF16)
        v_even = jnp.where(low, v_src, 0.0).astype(BF16)
        v_odd = jnp.where(low, 0.0, v_alt).astype(BF16)
        for n in range(tm // GROUP):
            kr = slice(n * GROUP, (n + 2) * GROUP)
            qr = slice(n * GROUP, (n + 1) * GROUP)
            keys = jnp.concatenate([k_even[kr], k_odd[kr]], axis=0)
            vals = jnp.concatenate([v_even[kr], v_odd[kr]], axis=0)
            col0 = kvh * Q_PER_KV * HEAD_DIM
            qs = jnp.concatenate(
                [q[qr, col0 + p * pair_w:col0 + (p + 1) * pair_w] for p in range(pairs)], axis=0)
            s = lax.dot_general(qs, keys, nt, preferred_element_type=F32)
            prob_rows, inv_den = [], []
            for p in range(pairs):
                halves, inv = [], []
                for e in range(2):
                    c0 = e * 2 * GROUP
                    s_prev = s[p * GROUP:(p + 1) * GROUP, c0:c0 + GROUP]
                    s_cur = s[p * GROUP:(p + 1) * GROUP, c0 + GROUP:c0 + 2 * GROUP]
                    if n == 0 and prev_fill is not None:
                        s_prev = s_prev + prev_fill
                    merged = jnp.where(use_cur, s_cur, s_prev)
                    sink = sinks_ref[0, kvh * Q_PER_KV + 2 * p + e]
                    m = jnp.maximum(jnp.max(merged, axis=1, keepdims=True), sink)
                    pe = jnp.exp(merged - m)
                    den = jnp.sum(pe, axis=1, keepdims=True) + jnp.exp(sink - m)
                    inv.append(1.0 / den)
                    halves += [jnp.where(use_cur, 0.0, pe).astype(BF16),
                               jnp.where(use_cur, pe, 0.0).astype(BF16)]
                prob_rows.append(jnp.concatenate(halves, axis=1))
                inv_den.append(jnp.where(low_q, inv[0], inv[1]))
            probs = jnp.concatenate(prob_rows, axis=0)
            o = _dot(probs, vals)
            for p in range(pairs):
                attn_ref[row0 + n * GROUP:row0 + (n + 1) * GROUP,
                         col0 + p * pair_w:col0 + (p + 1) * pair_w] = (
                    o[p * GROUP:(p + 1) * GROUP] * inv_den[p]).astype(BF16)

    mix = _dot(attn_ref[row0:row0 + tm, :], wo_ref[...])
    o_ref[row0:row0 + tm, :] = _layer_norm(alpha * x + mix, g_ref[1, 0:1, :], b_ref[1, 0:1, :])


def _attn_call(x2d, w_qkv, b_qkv, sinks, w_o, ln_g, ln_b, ffn_w_up, ffn_w_down, *, alpha, seq):
    n, d = x2d.shape
    d_q = w_o.shape[1]
    tm = ATTN_ROW_TILE
    kern = functools.partial(_attn_kernel, alpha=alpha, tiles_per_batch=seq // tm,
                             sub_tile=ATTN_SUB_TILE)
    cast_in, cast_out, cast_shapes = _weight_cast_stream(ffn_w_up, ffn_w_down, 1, n // tm)
    return pl.pallas_call(
        kern,
        grid=(n // tm,),
        in_specs=[
            pl.BlockSpec(memory_space=pltpu.SMEM),
            pl.BlockSpec((tm, d), lambda i: (i, 0)),
            _const_spec(w_qkv.shape[1:], lead=0),
            _const_spec(b_qkv.shape),
            _const_spec(w_o.shape[1:], lead=0),
            _const_spec(ln_g.shape),
            _const_spec(ln_b.shape),
        ] + cast_in,
        out_specs=[pl.BlockSpec((tm, d), lambda i: (i, 0))] + cast_out,
        out_shape=[jax.ShapeDtypeStruct((n, d), F32)] + cast_shapes,
        scratch_shapes=[
            pltpu.VMEM((tm, d_q), BF16),
            pltpu.VMEM((GROUP, 2 * N_KV_HEADS * HEAD_DIM), F32),
            pltpu.VMEM(w_qkv.shape[1:], BF16),
            pltpu.VMEM(w_o.shape[1:], BF16),
        ],
        compiler_params=pltpu.CompilerParams(
            dimension_semantics=("arbitrary",), vmem_limit_bytes=VMEM_LIMIT_BYTES),
        name="swa_sink_attention",
    )(sinks, x2d, w_qkv, b_qkv, w_o, ln_g, ln_b, ffn_w_up, ffn_w_down)


def kernel(x, ab_w_in, a_conv_w, a_conv_b, a_norm_g, a_norm_b, b_norm_g, b_norm_b, b_spatial_w, b_spatial_b, ab_w_out, c_w_qkv, c_b_qkv, c_sinks, c_w_o, ffn_w_up, ffn_conv_w, ffn_conv_b, ffn_w_down, ln_g, ln_b):
    bsz, seq, d = x.shape
    depth = ffn_w_up.shape[0]
    assert depth == 2 and all(seq % t == 0 for t in (FFN_ROW_TILE, ROW_TILE, ATTN_ROW_TILE))
    alpha = float(np.float32((2.0 * depth) ** 0.25))
    h = x.reshape(bsz * seq, d)
    h, w_up_bf, w_down_bf = _mixer_call(
        h, ab_w_in, a_conv_w, a_conv_b, a_norm_g, a_norm_b, b_norm_g, b_norm_b, b_spatial_w,
        b_spatial_b, ab_w_out, ln_g, ln_b, ffn_w_up, ffn_w_down, alpha=alpha, seq=seq)
    h = _ffn_call(h, w_up_bf, ffn_conv_w, ffn_conv_b, w_down_bf, ln_g, ln_b,
                  layer=0, alpha=alpha, seq=seq, natural_out=False)
    h, w_up_bf, w_down_bf = _attn_call(h, c_w_qkv, c_b_qkv, c_sinks, c_w_o, ln_g, ln_b,
                                       ffn_w_up, ffn_w_down, alpha=alpha, seq=seq)
    h = _ffn_call(h, w_up_bf, ffn_conv_w, ffn_conv_b, w_down_bf, ln_g, ln_b,
                  layer=1, alpha=alpha, seq=seq, natural_out=True)
    return h.reshape(bsz, seq, d)
```

```python
import functools
import math

import jax
import jax.numpy as jnp
import numpy as np
from jax import lax
from jax.experimental import pallas as pl
from jax.experimental.pallas import tpu as pltpu

F32 = jnp.float32
BF16 = jnp.bfloat16

LN_EPS = 1e-5
SUBLANES = 8
LANES = 128
BF16_ROWS = 16
GROUP = 128
VROWS = GROUP // SUBLANES
GMLP_GROUPS = 4
HEAD_DIM = 64
Q_PER_KV = 8
N_KV_HEADS = 2
VMEM_LIMIT_BYTES = 56 * 1024 * 1024

ROW_TILE = 512
ATTN_ROW_TILE = 1024
ATTN_SUB_TILE = 512
FFN_ROW_TILE = 1024
FFN_SUB_TILE = 512
FFN_COL_TILE = 256

GELU_C0 = math.sqrt(2.0 / math.pi)
GELU_C1 = 0.044715 * GELU_C0


def _layer_norm(z, g, b):
    mu = jnp.mean(z, axis=-1, keepdims=True)
    d = z - mu
    var = jnp.mean(d * d, axis=-1, keepdims=True)
    return d * lax.rsqrt(var + LN_EPS) * g + b


def _dot(a, b):
    return jnp.dot(a, b, preferred_element_type=F32)


def _one_plus_tanh_gelu_arg(x):
    return 1.0 + jnp.tanh(x * (GELU_C0 + GELU_C1 * (x * x)))


def _sigmoid(x):
    return 0.5 * (1.0 + jnp.tanh(0.5 * x))


def _time_of_row(rho):
    return VROWS * (rho & (SUBLANES - 1)) + (rho >> 3)


def _perm_index():
    rho = np.arange(GROUP)
    return VROWS * (rho % SUBLANES) + rho // SUBLANES


def _sublane_shift(cur, prev):
    row = lax.broadcasted_iota(jnp.int32, cur.shape, 0)
    return jnp.where(row == 0, pltpu.roll(prev, 1, axis=0), pltpu.roll(cur, 1, axis=0))


def _const_spec(shape, lead=None):
    nd = len(shape)
    if lead is None:
        return pl.BlockSpec(shape, lambda i: (0,) * nd, pipeline_mode=pl.Buffered(1))
    return pl.BlockSpec((None,) + tuple(shape), lambda i: (lead,) + (0,) * nd,
                        pipeline_mode=pl.Buffered(1))


def _weight_cast_stream(ffn_w_up, ffn_w_down, layer, n_steps):
    _, d, up_cols = ffn_w_up.shape
    _, d_ff, _ = ffn_w_down.shape
    assert d % (n_steps * BF16_ROWS) == 0
    up_rows = d // n_steps
    dn_blocks = max(nb for nb in range(1, n_steps + 1)
                    if d_ff % nb == 0 and (d_ff // nb) % BF16_ROWS == 0)
    dn_rows = d_ff // dn_blocks
    dn_index = lambda i: jnp.minimum(i, dn_blocks - 1)
    in_specs = [pl.BlockSpec((None, up_rows, up_cols), lambda i: (layer, i, 0)),
                pl.BlockSpec((None, dn_rows, d), lambda i: (layer, dn_index(i), 0))]
    out_specs = [pl.BlockSpec((up_rows, up_cols), lambda i: (i, 0)),
                 pl.BlockSpec((dn_rows, d), lambda i: (dn_index(i), 0))]
    out_shapes = [jax.ShapeDtypeStruct((d, up_cols), BF16), jax.ShapeDtypeStruct((d_ff, d), BF16)]
    return in_specs, out_specs, out_shapes


def _cast_weight_slabs(wup_in_ref, wdn_in_ref, wup_out_ref, wdn_out_ref):
    wup_out_ref[...] = wup_in_ref[...].astype(BF16)
    wdn_out_ref[...] = wdn_in_ref[...].astype(BF16)


def _ffn_rows(rows, x_ref, wup_ref, cw_ref, cb_ref, wdn_ref, g, b, o_ref, act_ref, carry_ref,
              stage_ref, norm_in, *, alpha, layer, d_ff, col_tile):
    tm = rows.stop - rows.start
    n_groups = tm // GROUP
    x = x_ref[rows, :]
    if norm_in is not None:
        x = _layer_norm(x, *norm_in)
    xb = x.astype(BF16)

    def conv_chunk(col, scale):
        h = _dot(xb, wup_ref[:, col:col + col_tile])
        w = cw_ref[:, col:col + col_tile]
        bias = cb_ref[layer:layer + 1, col:col + col_tile]
        if scale != 1.0:
            w, bias = w * scale, bias * scale
        prev = [carry_ref[0:SUBLANES, col:col + col_tile],
                carry_ref[SUBLANES:2 * SUBLANES, col:col + col_tile]]
        h1, h2 = [], []
        for gi in range(n_groups):
            hg = h[gi * GROUP:(gi + 1) * GROUP]
            cur = [hg[GROUP - 2 * SUBLANES:GROUP - SUBLANES], hg[GROUP - SUBLANES:]]
            wrapped = [_sublane_shift(cur[i], prev[i]) for i in range(2)]
            h1 += [wrapped[1], hg[:GROUP - SUBLANES]]
            h2 += [wrapped[0], wrapped[1], hg[:GROUP - 2 * SUBLANES]]
            prev = cur
        carry_ref[0:SUBLANES, col:col + col_tile] = prev[0]
        carry_ref[SUBLANES:2 * SUBLANES, col:col + col_tile] = prev[1]
        y = w[2:3] * h + bias
        y = y + w[1:2] * jnp.concatenate(h1, axis=0)
        y = y + w[0:1] * jnp.concatenate(h2, axis=0)
        return y

    for c in range(d_ff // col_tile):
        gate = conv_chunk(c * col_tile, 1.0)
        half_val = conv_chunk(d_ff + c * col_tile, 0.5)
        act = gate * _one_plus_tanh_gelu_arg(gate) * half_val
        act_ref[rows, c * col_tile:(c + 1) * col_tile] = act.astype(BF16)

    ffn = _dot(act_ref[rows, :], wdn_ref[...])
    y = _layer_norm(alpha * x + ffn, g, b)
    if stage_ref is None:
        o_ref[rows, :] = y
    else:
        d = y.shape[1]
        for c in range(d // LANES):
            stage_ref[c] = y[:, c * LANES:(c + 1) * LANES]
        for c in range(d // LANES):
            for gi in range(n_groups):
                for s in range(SUBLANES):
                    for half in range(VROWS // SUBLANES):
                        src = gi * GROUP + half * SUBLANES * SUBLANES + s
                        dst = rows.start + gi * GROUP + VROWS * s + SUBLANES * half
                        o_ref[dst:dst + SUBLANES, c * LANES:(c + 1) * LANES] = (
                            stage_ref[c, pl.ds(src, SUBLANES, stride=SUBLANES), :])


def _ffn_kernel(x_ref, wup_ref, cw_ref, cb_ref, wdn_ref, g_ref, b_ref, o_ref,
                act_ref, carry_ref, *rest, tiles_per_batch, sub_tile, **static):
    tm = x_ref.shape[0]

    @pl.when(pl.program_id(0) % tiles_per_batch == 0)
    def _():
        carry_ref[...] = jnp.zeros_like(carry_ref)

    layer = static["layer"]
    g = g_ref[layer, 1:2, :]
    b = b_ref[layer, 1:2, :]
    stage_ref = rest[0] if rest else None
    norm_in = (g_ref[0, 0:1, :], b_ref[0, 0:1, :]) if layer == 0 else None
    for t in range(tm // sub_tile):
        _ffn_rows(slice(t * sub_tile, (t + 1) * sub_tile), x_ref, wup_ref, cw_ref, cb_ref, wdn_ref,
                  g, b, o_ref, act_ref, carry_ref, stage_ref, norm_in, **static)


def _ffn_call(x2d, w_up_bf, conv_w, conv_b, w_down_bf, ln_g, ln_b, *, layer, alpha, seq,
              natural_out):
    n, d = x2d.shape
    d_ff = w_down_bf.shape[0]
    tm = FFN_ROW_TILE
    kern = functools.partial(_ffn_kernel, alpha=alpha, layer=layer, tiles_per_batch=seq // tm,
                             sub_tile=FFN_SUB_TILE, d_ff=d_ff, col_tile=FFN_COL_TILE)
    scratch = [pltpu.VMEM((tm, d_ff), BF16), pltpu.VMEM((2 * SUBLANES, 2 * d_ff), F32)]
    if natural_out:
        scratch.append(pltpu.VMEM((d // LANES, FFN_SUB_TILE, LANES), F32))
    return pl.pallas_call(
        kern,
        grid=(n // tm,),
        in_specs=[
            pl.BlockSpec((tm, d), lambda i: (i, 0)),
            _const_spec((d, 2 * d_ff)),
            _const_spec((3, 2 * d_ff), lead=layer),
            _const_spec(conv_b.shape),
            _const_spec((d_ff, d)),
            _const_spec(ln_g.shape),
            _const_spec(ln_b.shape),
        ],
        out_specs=pl.BlockSpec((tm, d), lambda i: (i, 0)),
        out_shape=jax.ShapeDtypeStruct((n, d), F32),
        scratch_shapes=scratch,
        compiler_params=pltpu.CompilerParams(
            dimension_semantics=("arbitrary",), vmem_limit_bytes=VMEM_LIMIT_BYTES),
        name="conv_ffn_out" if natural_out else "conv_ffn",
    )(x2d, w_up_bf, conv_w, conv_b, w_down_bf, ln_g, ln_b)


def _mixer_kernel(*refs, alpha, tiles_per_batch, width, n_col_blocks):
    x_refs = refs[:n_col_blocks]
    (win_f32_ref, acw_ref, acb_ref, ang_ref, anb_ref, bng_ref, bnb_ref,
     ws_ref, sb_ref, wout_f32_ref, g_ref, b_ref, ffn_up_ref, ffn_dn_ref,
     o_ref, ffn_up_bf_ref, ffn_dn_bf_ref, carry_ref, win_ref, wout_ref) = refs[n_col_blocks:]
    tm = o_ref.shape[0]
    n_groups = tm // GROUP
    _cast_weight_slabs(ffn_up_ref, ffn_dn_ref, ffn_up_bf_ref, ffn_dn_bf_ref)

    @pl.when(pl.program_id(0) == 0)
    def _():
        win_ref[...] = win_f32_ref[...].astype(BF16)
        wout_ref[...] = wout_f32_ref[...].astype(BF16)

    @pl.when(pl.program_id(0) % tiles_per_batch == 0)
    def _():
        carry_ref[...] = jnp.zeros_like(carry_ref)

    x = jnp.concatenate(
        [jnp.concatenate(
            [x_refs[c][pl.ds(g * GROUP + j, SUBLANES, stride=VROWS), :]
             for g in range(n_groups) for j in range(VROWS)], axis=0)
         for c in range(n_col_blocks)], axis=1)
    xb = x.astype(BF16)
    half_w = width // 2
    a_halves = []
    for hb in range(2):
        a_val = _dot(xb, win_ref[:, hb * half_w:(hb + 1) * half_w])
        a_gate = _dot(xb, win_ref[:, width + hb * half_w:width + (hb + 1) * half_w])
        a_halves.append(a_val * _sigmoid(a_gate))
    a = jnp.concatenate(a_halves, axis=1)
    b_v = _dot(xb, win_ref[:, 3 * width:4 * width])
    b_u = _dot(xb, win_ref[:, 2 * width:3 * width])
    ext = jnp.concatenate([carry_ref[...], a], axis=0)
    carry_ref[...] = a[tm - GROUP:]
    n_taps = acw_ref.shape[0]
    max_sub = (n_taps - 1) // VROWS + 1

    def sublane_shifted(z):
        out = []
        for g in range(n_groups + 1):
            zg = z[g * GROUP:(g + 1) * GROUP]
            zp = z[(g - 1) * GROUP:g * GROUP] if g > 0 else zg
            out += [_sublane_shift(zg[j * SUBLANES:(j + 1) * SUBLANES],
                                   zp[j * SUBLANES:(j + 1) * SUBLANES]) for j in range(VROWS)]
        return jnp.concatenate(out, axis=0)

    shifted = [ext]
    for _ in range(max_sub):
        shifted.append(sublane_shifted(shifted[-1]))
    acw = acw_ref[...]
    conv = jnp.broadcast_to(acb_ref[...], (tm, width))
    for k in range(n_taps):
        q, r = divmod(k, VROWS)
        lo = shifted[q][GROUP:]
        if r == 0:
            term = lo
        else:
            hi = shifted[q + 1][GROUP:]
            cut = (VROWS - r) * SUBLANES
            parts = []
            for g in range(n_groups):
                parts += [hi[g * GROUP + cut:(g + 1) * GROUP], lo[g * GROUP:g * GROUP + cut]]
            term = jnp.concatenate(parts, axis=0)
        j = n_taps - 1 - k
        conv = conv + acw[j:j + 1] * term
    a_out = _layer_norm(conv, ang_ref[...], anb_ref[...])
    a_out = a_out * _sigmoid(a_out)

    gelu_v = 0.5 * b_v * _one_plus_tanh_gelu_arg(b_v)
    v = _layer_norm(gelu_v, bng_ref[...], bnb_ref[...]).astype(BF16)
    ti = _time_of_row(lax.broadcasted_iota(jnp.int32, (GROUP, GROUP), 0))
    tj = _time_of_row(lax.broadcasted_iota(jnp.int32, (GROUP, GROUP), 1))
    gd = width // GMLP_GROUPS
    ws = [(jnp.where(ti >= tj, ws_ref[g], 0.0) * 0.5).astype(BF16) for g in range(GMLP_GROUPS)]
    half_bias = sb_ref[...] * 0.5
    chunks = []
    for c in range(n_groups):
        rows = slice(c * GROUP, (c + 1) * GROUP)
        cols = [_dot(ws[g], v[rows, g * gd:(g + 1) * gd]) for g in range(GMLP_GROUPS)]
        chunks.append(jnp.concatenate(cols, axis=1) + half_bias)
    b_out = b_u * _one_plus_tanh_gelu_arg(b_u) * jnp.concatenate(chunks, axis=0)

    mix = _dot(jnp.concatenate([a_out.astype(BF16), b_out.astype(BF16)], axis=1), wout_ref[...])
    o_ref[...] = alpha * x + mix


def _mixer_call(x2d, w_in, a_conv_w, a_conv_b, a_norm_g, a_norm_b, b_norm_g, b_norm_b,
                b_spatial_w, b_spatial_b, w_out, ln_g, ln_b, ffn_w_up, ffn_w_down, *, alpha, seq):
    n, d = x2d.shape
    width = a_conv_w.shape[2]
    gd = width // GMLP_GROUPS
    tm = ROW_TILE
    n_col_blocks = d // LANES
    perm = _perm_index()
    ws_perm = b_spatial_w[0][:, perm][:, :, perm]
    sbias = jnp.repeat(b_spatial_b[0].T[perm], gd, axis=1)
    kern = functools.partial(_mixer_kernel, alpha=alpha, tiles_per_batch=seq // tm, width=width,
                             n_col_blocks=n_col_blocks)
    x_specs = [pl.BlockSpec((tm, LANES), functools.partial(lambda c, i: (i, c), c))
               for c in range(n_col_blocks)]
    cast_in, cast_out, cast_shapes = _weight_cast_stream(ffn_w_up, ffn_w_down, 0, n // tm)
    return pl.pallas_call(
        kern,
        grid=(n // tm,),
        in_specs=x_specs + [
            _const_spec(w_in.shape[1:], lead=0),
            _const_spec(a_conv_w.shape[1:], lead=0),
            _const_spec(a_conv_b.shape),
            _const_spec(a_norm_g.shape),
            _const_spec(a_norm_b.shape),
            _const_spec(b_norm_g.shape),
            _const_spec(b_norm_b.shape),
            _const_spec(ws_perm.shape),
            _const_spec(sbias.shape),
            _const_spec(w_out.shape[1:], lead=0),
            _const_spec(ln_g.shape),
            _const_spec(ln_b.shape),
        ] + cast_in,
        out_specs=[pl.BlockSpec((tm, d), lambda i: (i, 0))] + cast_out,
        out_shape=[jax.ShapeDtypeStruct((n, d), F32)] + cast_shapes,
        scratch_shapes=[pltpu.VMEM((GROUP, width), F32),
                        pltpu.VMEM(w_in.shape[1:], BF16),
                        pltpu.VMEM(w_out.shape[1:], BF16)],
        compiler_params=pltpu.CompilerParams(
            dimension_semantics=("arbitrary",), vmem_limit_bytes=VMEM_LIMIT_BYTES),
        name="conv_gmlp_mixer",
    )(*([x2d] * n_col_blocks), w_in, a_conv_w, a_conv_b, a_norm_g, a_norm_b, b_norm_g, b_norm_b,
      ws_perm, sbias, w_out, ln_g, ln_b, ffn_w_up, ffn_w_down)


def _attn_kernel(sinks_ref, x_ref, wqkv_f32_ref, bqkv_ref, wo_f32_ref, g_ref, b_ref,
                 ffn_up_ref, ffn_dn_ref, o_ref, ffn_up_bf_ref, ffn_dn_bf_ref,
                 attn_ref, kv_carry_ref, wqkv_ref, wo_ref, *, alpha, tiles_per_batch, sub_tile):
    _cast_weight_slabs(ffn_up_ref, ffn_dn_ref, ffn_up_bf_ref, ffn_dn_bf_ref)
    first = pl.program_id(0) % tiles_per_batch == 0

    @pl.when(first)
    def _():
        kv_carry_ref[...] = jnp.zeros_like(kv_carry_ref)

    @pl.when(pl.program_id(0) == 0)
    def _():
        wqkv_ref[...] = wqkv_f32_ref[...].astype(BF16)
        wo_ref[...] = wo_f32_ref[...].astype(BF16)

    for t in range(x_ref.shape[0] // sub_tile):
        _attn_rows(t * sub_tile, sub_tile, first if t == 0 else None, sinks_ref, x_ref, bqkv_ref,
                   g_ref, b_ref, o_ref, attn_ref, kv_carry_ref, wqkv_ref, wo_ref, alpha=alpha)


def _attn_rows(row0, tm, first, sinks_ref, x_ref, bqkv_ref, g_ref, b_ref, o_ref,
               attn_ref, kv_carry_ref, wqkv_ref, wo_ref, *, alpha):
    d_q = N_KV_HEADS * Q_PER_KV * HEAD_DIM
    d_kv = N_KV_HEADS * HEAD_DIM
    pair_w = 2 * HEAD_DIM
    pairs = Q_PER_KV // 2
    x = x_ref[row0:row0 + tm, :]
    xb = x.astype(BF16)
    kv = _dot(xb, wqkv_ref[:, d_q:]) + bqkv_ref[:, d_q:]
    q_w = d_q // N_KV_HEADS
    q = jnp.concatenate(
        [((_dot(xb, wqkv_ref[:, h * q_w:(h + 1) * q_w]) + bqkv_ref[:, h * q_w:(h + 1) * q_w])
          * (1.0 / math.sqrt(HEAD_DIM))).astype(BF16) for h in range(N_KV_HEADS)], axis=1)
    kv_all = jnp.concatenate([kv_carry_ref[...], kv], axis=0)
    kv_carry_ref[...] = kv[tm - GROUP:]
    k_all = kv_all[:, :d_kv]
    v_all = kv_all[:, d_kv:]
    k_sw = pltpu.roll(k_all, HEAD_DIM, axis=1)
    v_sw = pltpu.roll(v_all, HEAD_DIM, axis=1)
    low = lax.broadcasted_iota(jnp.int32, k_all.shape, 1) < HEAD_DIM
    low_q = lax.broadcasted_iota(jnp.int32, (GROUP, pair_w), 1) < HEAD_DIM

    tq = _time_of_row(lax.broadcasted_iota(jnp.int32, (GROUP, GROUP), 0))
    tk = _time_of_row(lax.broadcasted_iota(jnp.int32, (GROUP, GROUP), 1))
    use_cur = tk <= tq
    prev_fill = None if first is None else jnp.where(first, -jnp.inf, 0.0).astype(F32)
    nt = (((1,), (1,)), ((), ()))

    for kvh in range(N_KV_HEADS):
        k_src, k_alt = (k_all, k_sw) if kvh == 0 else (k_sw, k_all)
        v_src, v_alt = (v_all, v_sw) if kvh == 0 else (v_sw, v_all)
        k_even = jnp.where(low, k_src, 0.0).astype(BF16)
        k_odd = jnp.where(low, 0.0, k_alt).astype(BF16)
        v_even = jnp.where(low, v_src, 0.0).astype(BF16)
        v_odd = jnp.where(low, 0.0, v_alt).astype(BF16)
        for n in range(tm // GROUP):
            kr = slice(n * GROUP, (n + 2) * GROUP)
            qr = slice(n * GROUP, (n + 1) * GROUP)
            keys = jnp.concatenate([k_even[kr], k_odd[kr]], axis=0)
            vals = jnp.concatenate([v_even[kr], v_odd[kr]], axis=0)
            col0 = kvh * Q_PER_KV * HEAD_DIM
            qs = jnp.concatenate(
                [q[qr, col0 + p * pair_w:col0 + (p + 1) * pair_w] for p in range(pairs)], axis=0)
            s = lax.dot_general(qs, keys, nt, preferred_element_type=F32)
            prob_rows, inv_den = [], []
            for p in range(pairs):
                halves, inv = [], []
                for e in range(2):
                    c0 = e * 2 * GROUP
                    s_prev = s[p * GROUP:(p + 1) * GROUP, c0:c0 + GROUP]
                    s_cur = s[p * GROUP:(p + 1) * GROUP, c0 + GROUP:c0 + 2 * GROUP]
                    if n == 0 and prev_fill is not None:
                        s_prev = s_prev + prev_fill
                    merged = jnp.where(use_cur, s_cur, s_prev)
                    sink = sinks_ref[0, kvh * Q_PER_KV + 2 * p + e]
                    m = jnp.maximum(jnp.max(merged, axis=1, keepdims=True), sink)
                    pe = jnp.exp(merged - m)
                    den = jnp.sum(pe, axis=1, keepdims=True) + jnp.exp(sink - m)
                    inv.append(1.0 / den)
                    halves += [jnp.where(use_cur, 0.0, pe).astype(BF16),
                               jnp.where(use_cur, pe, 0.0).astype(BF16)]
                prob_rows.append(jnp.concatenate(halves, axis=1))
                inv_den.append(jnp.where(low_q, inv[0], inv[1]))
            probs = jnp.concatenate(prob_rows, axis=0)
            o = _dot(probs, vals)
            for p in range(pairs):
                attn_ref[row0 + n * GROUP:row0 + (n + 1) * GROUP,
                         col0 + p * pair_w:col0 + (p + 1) * pair_w] = (
                    o[p * GROUP:(p + 1) * GROUP] * inv_den[p]).astype(BF16)

    mix = _dot(attn_ref[row0:row0 + tm, :], wo_ref[...])
    o_ref[row0:row0 + tm, :] = _layer_norm(alpha * x + mix, g_ref[1, 0:1, :], b_ref[1, 0:1, :])


def _attn_call(x2d, w_qkv, b_qkv, sinks, w_o, ln_g, ln_b, ffn_w_up, ffn_w_down, *, alpha, seq):
    n, d = x2d.shape
    d_q = w_o.shape[1]
    tm = ATTN_ROW_TILE
    kern = functools.partial(_attn_kernel, alpha=alpha, tiles_per_batch=seq // tm,
                             sub_tile=ATTN_SUB_TILE)
    cast_in, cast_out, cast_shapes = _weight_cast_stream(ffn_w_up, ffn_w_down, 1, n // tm)
    return pl.pallas_call(
        kern,
        grid=(n // tm,),
        in_specs=[
            pl.BlockSpec(memory_space=pltpu.SMEM),
            pl.BlockSpec((tm, d), lambda i: (i, 0)),
            _const_spec(w_qkv.shape[1:], lead=0),
            _const_spec(b_qkv.shape),
            _const_spec(w_o.shape[1:], lead=0),
            _const_spec(ln_g.shape),
            _const_spec(ln_b.shape),
        ] + cast_in,
        out_specs=[pl.BlockSpec((tm, d), lambda i: (i, 0))] + cast_out,
        out_shape=[jax.ShapeDtypeStruct((n, d), F32)] + cast_shapes,
        scratch_shapes=[
            pltpu.VMEM((tm, d_q), BF16),
            pltpu.VMEM((GROUP, 2 * N_KV_HEADS * HEAD_DIM), F32),
            pltpu.VMEM(w_qkv.shape[1:], BF16),
            pltpu.VMEM(w_o.shape[1:], BF16),
        ],
        compiler_params=pltpu.CompilerParams(
            dimension_semantics=("arbitrary",), vmem_limit_bytes=VMEM_LIMIT_BYTES),
        name="swa_sink_attention",
    )(sinks, x2d, w_qkv, b_qkv, w_o, ln_g, ln_b, ffn_w_up, ffn_w_down)


def kernel(x, ab_w_in, a_conv_w, a_conv_b, a_norm_g, a_norm_b, b_norm_g, b_norm_b, b_spatial_w, b_spatial_b, ab_w_out, c_w_qkv, c_b_qkv, c_sinks, c_w_o, ffn_w_up, ffn_conv_w, ffn_conv_b, ffn_w_down, ln_g, ln_b):
    bsz, seq, d = x.shape
    depth = ffn_w_up.shape[0]
    assert depth == 2 and all(seq % t == 0 for t in (FFN_ROW_TILE, ROW_TILE, ATTN_ROW_TILE))
    alpha = float(np.float32((2.0 * depth) ** 0.25))
    h = x.reshape(bsz * seq, d)
    h, w_up_bf, w_down_bf = _mixer_call(
        h, ab_w_in, a_conv_w, a_conv_b, a_norm_g, a_norm_b, b_norm_g, b_norm_b, b_spatial_w,
        b_spatial_b, ab_w_out, ln_g, ln_b, ffn_w_up, ffn_w_down, alpha=alpha, seq=seq)
    h = _ffn_call(h, w_up_bf, ffn_conv_w, ffn_conv_b, w_down_bf, ln_g, ln_b,
                  layer=0, alpha=alpha, seq=seq, natural_out=False)
    h, w_up_bf, w_down_bf = _attn_call(h, c_w_qkv, c_b_qkv, c_sinks, c_w_o, ln_g, ln_b,
                                       ffn_w_up, ffn_w_down, alpha=alpha, seq=seq)
    h = _ffn_call(h, w_up_bf, ffn_conv_w, ffn_conv_b, w_down_bf, ln_g, ln_b,
                  layer=1, alpha=alpha, seq=seq, natural_out=True)
    return h.reshape(bsz, seq, d)
```
